```python
import math
import jax, jax.numpy as jnp
from jax import lax
import numpy as np

D_MODEL = 1024
BATCH = 32
SEQ = 2048
DEPTH = 4
DEC_BATCH = 8
DEC_SEQ = 2048
PAST_LEN = 128

N_MIXERS = 2
N_CONV_LAYERS = (DEPTH + 1) // 2
N_SSM_LAYERS = DEPTH // 2
SSM_GROUP = 16
SSM_GROUPS = D_MODEL // SSM_GROUP
SSM_STATE = 64
N_DIR = 2
FFN_HIDDEN = 2816
DEEPNORM_ALPHA = (2 * DEPTH) ** 0.25
DEEPNORM_BETA = (8 * DEPTH) ** -0.25
LN_EPS = 1e-5
DT_MIN = 1e-3
DT_MAX = 1e-1

kernel_name = "hybrid_shortconv_s5_convffn_encoder"


def _layer_norm(x, g, b):
    xf = x.astype(jnp.float32)
    mu = jnp.mean(xf, axis=-1, keepdims=True)
    var = jnp.mean(jnp.square(xf - mu), axis=-1, keepdims=True)
    y = (xf - mu) * lax.rsqrt(var + LN_EPS)
    return (y * g.astype(jnp.float32) + b.astype(jnp.float32)).astype(x.dtype)


def _conv3(x, w, b):
    xp = jnp.pad(x, ((0, 0), (1, 1), (0, 0)))
    return xp[:, :-2] * w[0] + xp[:, 1:-1] * w[1] + xp[:, 2:] * w[2] + b


def _short_conv_mixer(u, w_in, conv_w, conv_b, w_out):
    bgate, cgate, h = jnp.split(u @ w_in, 3, axis=-1)
    return (bgate * _conv3(cgate * h, conv_w, conv_b)) @ w_out


def _cmul(ar, ai, br, bi):
    return ar * br - ai * bi, ar * bi + ai * br


def _scan_combine(e1, e2):
    a1r, a1i, b1r, b1i = e1
    a2r, a2i, b2r, b2i = e2
    ar, ai = _cmul(a1r, a1i, a2r, a2i)
    br, bi = _cmul(a2r, a2i, b1r, b1i)
    return ar, ai, br + b2r, bi + b2i


def _s5_sequence(u, a_re, a_im, log_dt, b_re, b_im, c_re, c_im, d_skip):
    f32 = jnp.float32
    L = u.shape[0]
    uf = u.astype(f32)
    ug = uf.reshape(L, SSM_GROUPS, SSM_GROUP)
    y = d_skip.astype(f32) * uf
    for k in range(N_DIR):
        lam_re = a_re[k].astype(f32)
        lam_im = a_im[k].astype(f32)
        dt = jnp.exp(log_dt[k].astype(f32))[:, None]
        mag = jnp.exp(lam_re * dt)
        ang = lam_im * dt
        lb_re = mag * jnp.cos(ang)
        lb_im = mag * jnp.sin(ang)
        den = lam_re * lam_re + lam_im * lam_im
        f_re = ((lb_re - 1.0) * lam_re + lb_im * lam_im) / den
        f_im = (lb_im * lam_re - (lb_re - 1.0) * lam_im) / den
        bu_re = jnp.einsum('lgi,gpi->lgp', ug, b_re[k].astype(f32))
        bu_im = jnp.einsum('lgi,gpi->lgp', ug, b_im[k].astype(f32))
        v_re, v_im = _cmul(f_re, f_im, bu_re, bu_im)
        if k == 1:
            v_re = jnp.flip(v_re, axis=0)
            v_im = jnp.flip(v_im, axis=0)
        shape = v_re.shape
        _, _, s_re, s_im = lax.associative_scan(
            _scan_combine,
            (jnp.broadcast_to(lb_re, shape), jnp.broadcast_to(lb_im, shape), v_re, v_im),
            axis=0)
        if k == 1:
            s_re = jnp.flip(s_re, axis=0)
            s_im = jnp.flip(s_im, axis=0)
        y_dir = (jnp.einsum('lgp,gip->lgi', s_re, c_re[k].astype(f32))
                 - jnp.einsum('lgp,gip->lgi', s_im, c_im[k].astype(f32)))
        y = y + y_dir.reshape(L, D_MODEL)
    return y.astype(u.dtype)


def _s5_mixer(u, a_re, a_im, log_dt, b_re, b_im, c_re, c_im, d_skip, w_glu):
    y = lax.map(lambda us: _s5_sequence(us, a_re, a_im, log_dt, b_re, b_im, c_re, c_im, d_skip), u)
    z = jax.nn.gelu(y)
    val, gate = jnp.split(z @ w_glu, 2, axis=-1)
    return val * jax.nn.sigmoid(gate)


def _conv_ffn(u, w_up, conv_w, conv_b, w_down):
    a, v = jnp.split(_conv3(u @ w_up, conv_w, conv_b), 2, axis=-1)
    return (jax.nn.gelu(a) * v) @ w_down


def _trunk(x, c, ada_w, ada_b, ln1_g, ln1_b, ln2_g, ln2_b,
           sc_w_in, sc_conv_w, sc_conv_b, sc_w_out,
           s5_a_re, s5_a_im, s5_log_dt, s5_b_re, s5_b_im, s5_c_re, s5_c_im, s5_d, s5_w_glu,
           ffn_w_up, ffn_conv_w, ffn_conv_b, ffn_w_down):
    c_act = jax.nn.silu(c)
    for i in range(DEPTH):
        mod = (c_act @ ada_w[i] + ada_b[i])[:, None, :]
        sh1, sc1, g1, sh2, sc2, g2 = jnp.split(mod, 6, axis=-1)
        j = i // N_MIXERS
        u = x * (1.0 + sc1) + sh1
        if i % N_MIXERS == 0:
            m = _short_conv_mixer(u, sc_w_in[j], sc_conv_w[j], sc_conv_b[j], sc_w_out[j])
        else:
            m = _s5_mixer(u, s5_a_re[j], s5_a_im[j], s5_log_dt[j], s5_b_re[j], s5_b_im[j],
                          s5_c_re[j], s5_c_im[j], s5_d[j], s5_w_glu[j])
        x = _layer_norm(DEEPNORM_ALPHA * x + (1.0 + g1) * m, ln1_g[i], ln1_b[i])
        u = x * (1.0 + sc2) + sh2
        f = _conv_ffn(u, ffn_w_up[i], ffn_conv_w[i], ffn_conv_b[i], ffn_w_down[i])
        x = _layer_norm(DEEPNORM_ALPHA * x + (1.0 + g2) * f, ln2_g[i], ln2_b[i])
    return x


def setup_inputs(seed: int = 0) -> dict:
    key = jax.random.key(seed)
    ks = jax.random.split(key, 28)
    f32 = jnp.float32
    D, F, G, P, I = D_MODEL, FFN_HIDDEN, SSM_GROUPS, SSM_STATE, SSM_GROUP
    NC, NS = N_CONV_LAYERS, N_SSM_LAYERS

    def nrm(k, shape, s):
        return jax.random.normal(k, shape, f32) * s

    n_idx = jnp.arange(P, dtype=f32)
    return {
        "x_prompt": nrm(ks[0], (BATCH, SEQ, D), 1.0),
        "x_sample": nrm(ks[1], (DEC_BATCH, DEC_SEQ, D), 1.0),
        "c_prompt": nrm(ks[2], (BATCH, D), 1.0),
        "c_sample": nrm(ks[3], (DEC_BATCH, D), 1.0),
        "ada_w": nrm(ks[4], (DEPTH, D, 6 * D), 0.1 * D ** -0.5),
        "ada_b": nrm(ks[5], (DEPTH, 6 * D), 0.01),
        "ln1_g": 1.0 + nrm(ks[6], (DEPTH, D), 0.02),
        "ln1_b": nrm(ks[7], (DEPTH, D), 0.02),
        "ln2_g": 1.0 + nrm(ks[8], (DEPTH, D), 0.02),
        "ln2_b": nrm(ks[9], (DEPTH, D), 0.02),
        "sc_w_in": nrm(ks[10], (NC, D, 3 * D), D ** -0.5),
        "sc_conv_w": nrm(ks[11], (NC, 3, D), 3 ** -0.5),
        "sc_conv_b": nrm(ks[12], (NC, D), 0.02),
        "sc_w_out": nrm(ks[13], (NC, D, D), DEEPNORM_BETA * D ** -0.5),
        "s5_a_re": -0.5 + nrm(ks[14], (NS, N_DIR, G, P), 0.01),
        "s5_a_im": math.pi * n_idx + nrm(ks[15], (NS, N_DIR, G, P), 0.01),
        "s5_log_dt": jax.random.uniform(ks[16], (NS, N_DIR, G), f32,
                                        math.log(DT_MIN), math.log(DT_MAX)),
        "s5_b_re": nrm(ks[17], (NS, N_DIR, G, P, I), (2 * I) ** -0.5),
        "s5_b_im": nrm(ks[18], (NS, N_DIR, G, P, I), (2 * I) ** -0.5),
        "s5_c_re": nrm(ks[19], (NS, N_DIR, G, I, P), P ** -0.5),
        "s5_c_im": nrm(ks[20], (NS, N_DIR, G, I, P), P ** -0.5),
        "s5_d": nrm(ks[21], (NS, D), 1.0),
        "s5_w_glu": nrm(ks[22], (NS, D, 2 * D), DEEPNORM_BETA * D ** -0.5),
        "ffn_w_up": nrm(ks[23], (DEPTH, D, 2 * F), D ** -0.5),
        "ffn_conv_w": nrm(ks[24], (DEPTH, 3, 2 * F), 3 ** -0.5),
        "ffn_conv_b": nrm(ks[25], (DEPTH, 2 * F), 0.02),
        "ffn_w_down": nrm(ks[26], (DEPTH, F, D), DEEPNORM_BETA * F ** -0.5),
    }


def reference(x_prompt, x_sample, c_prompt, c_sample, ada_w, ada_b, ln1_g, ln1_b, ln2_g, ln2_b,
              sc_w_in, sc_conv_w, sc_conv_b, sc_w_out,
              s5_a_re, s5_a_im, s5_log_dt, s5_b_re, s5_b_im, s5_c_re, s5_c_im, s5_d, s5_w_glu,
              ffn_w_up, ffn_conv_w, ffn_conv_b, ffn_w_down):
    y_prompt = _trunk(x_prompt, c_prompt, ada_w, ada_b, ln1_g, ln1_b, ln2_g, ln2_b,
                      sc_w_in, sc_conv_w, sc_conv_b, sc_w_out,
                      s5_a_re, s5_a_im, s5_log_dt, s5_b_re, s5_b_im, s5_c_re, s5_c_im, s5_d, s5_w_glu,
                      ffn_w_up, ffn_conv_w, ffn_conv_b, ffn_w_down)
    y_sample = _trunk(x_sample, c_sample, ada_w, ada_b, ln1_g, ln1_b, ln2_g, ln2_b,
                      sc_w_in, sc_conv_w, sc_conv_b, sc_w_out,
                      s5_a_re, s5_a_im, s5_log_dt, s5_b_re, s5_b_im, s5_c_re, s5_c_im, s5_d, s5_w_glu,
                      ffn_w_up, ffn_conv_w, ffn_conv_b, ffn_w_down)
    return (y_prompt, y_sample)
```

```python
import functools
import math

import jax
import jax.numpy as jnp
from jax import lax
from jax.experimental import pallas as pl
from jax.experimental.pallas import tpu as pltpu

D_MODEL = 1024
DEPTH = 4
FFN_HIDDEN = 2816
SSM_GROUP = 16
SSM_GROUPS = D_MODEL // SSM_GROUP
SSM_STATE = 64
DEEPNORM_ALPHA = (2 * DEPTH) ** 0.25
LN_EPS = 1e-5

SUBLANES = 8
CHUNK = 16
CHUNK_COLS = CHUNK * SSM_GROUP
ROW_TILE = 512
FFN_COL_CHUNKS = (768, 768, 768, 512)
VMEM_LIMIT = 56 * 1024 * 1024

_HI = lax.Precision.HIGHEST
_BF = jnp.bfloat16
_F32 = jnp.float32


def _const_spec(shape):
    nd = len(shape)
    return pl.BlockSpec(shape, lambda *_: (0,) * nd, pipeline_mode=pl.Buffered(1))


def _gelu(x):
    c = math.sqrt(2.0 / math.pi)
    return 0.5 * x * (1.0 + jnp.tanh(c * (x + 0.044715 * (x * x * x))))


def _layer_norm(h, g, b):
    mu = jnp.mean(h, axis=-1, keepdims=True)
    d = h - mu
    var = jnp.mean(d * d, axis=-1, keepdims=True)
    return d * lax.rsqrt(var + LN_EPS) * g + b


def _conv3_rows(buf_ref, cur, before, after, w, b, first, last):
    rows = cur.shape[0]
    zero = jnp.zeros_like(before)
    buf_ref[SUBLANES:SUBLANES + rows, :] = cur
    buf_ref[SUBLANES - 1:SUBLANES, :] = jnp.where(first, zero, before)
    buf_ref[SUBLANES + rows:SUBLANES + rows + 1, :] = jnp.where(last, zero, after)
    up = buf_ref[SUBLANES - 1:SUBLANES - 1 + rows, :]
    dn = buf_ref[SUBLANES + 1:SUBLANES + 1 + rows, :]
    return up * w[0:1, :] + cur * w[1:2, :] + dn * w[2:3, :] + b


def _ada_kernel(c_ref, w_ref, b_ref, o_ref):
    c = c_ref[...]
    ca = c * jax.nn.sigmoid(c)
    o_ref[...] = jnp.dot(ca, w_ref[...], precision=_HI, preferred_element_type=_F32) + b_ref[...]


def _ada(c, ada_w, ada_b):
    bsz = c.shape[0]
    out = pl.pallas_call(
        _ada_kernel,
        grid=(DEPTH, 6),
        in_specs=[
            pl.BlockSpec((bsz, D_MODEL), lambda i, n: (0, 0)),
            pl.BlockSpec((None, D_MODEL, D_MODEL), lambda i, n: (i, 0, n)),
            pl.BlockSpec((None, 1, D_MODEL), lambda i, n: (i, 0, n)),
        ],
        out_specs=pl.BlockSpec((None, bsz, D_MODEL), lambda i, n: (i, 0, n)),
        out_shape=jax.ShapeDtypeStruct((DEPTH, bsz, 6 * D_MODEL), _F32),
        compiler_params=pltpu.CompilerParams(
            dimension_semantics=("arbitrary", "arbitrary"), vmem_limit_bytes=VMEM_LIMIT),
        name="ada_mod",
    )(c, ada_w, ada_b.reshape(DEPTH, 1, 6 * D_MODEL))
    return out.reshape(DEPTH, bsz, 6, D_MODEL)


def _row_specs(seq_len, tl):
    nb8 = seq_len // SUBLANES
    per = tl // SUBLANES
    main = pl.BlockSpec((None, tl, D_MODEL), lambda b, t: (b, t, 0))
    prev = pl.BlockSpec((None, SUBLANES, D_MODEL),
                        lambda b, t: (b, jnp.maximum(t * per - 1, 0), 0))
    nxt = pl.BlockSpec((None, SUBLANES, D_MODEL),
                       lambda b, t: (b, jnp.minimum((t + 1) * per, nb8 - 1), 0))
    return main, prev, nxt


def _mod_spec():
    return pl.BlockSpec((None, 6, D_MODEL), lambda b, t: (b, 0, 0))


def _conv_mixer_kernel(x_ref, xp_ref, xn_ref, mod_ref, win_ref, cw_ref, cb_ref, wout_ref,
                       g_ref, b_ref, o_ref, buf_ref):
    t = pl.program_id(1)
    first = t == 0
    last = t == pl.num_programs(1) - 1
    sh = mod_ref[0:1, :]
    sc = 1.0 + mod_ref[1:2, :]
    gate = 1.0 + mod_ref[2:3, :]
    x = x_ref[...]
    u = (x * sc + sh).astype(_BF)
    p = jnp.dot(u, win_ref[...], preferred_element_type=_F32)
    q = p[:, D_MODEL:2 * D_MODEL] * p[:, 2 * D_MODEL:]
    uh = jnp.concatenate([xp_ref[...], xn_ref[...]], axis=0)
    uh = (uh * sc + sh).astype(_BF)
    ph = jnp.dot(uh, win_ref[:, D_MODEL:], preferred_element_type=_F32)
    qh = ph[:, :D_MODEL] * ph[:, D_MODEL:]
    cq = _conv3_rows(buf_ref, q, qh[SUBLANES - 1:SUBLANES, :], qh[SUBLANES:SUBLANES + 1, :],
                     cw_ref[...], cb_ref[...], first, last)
    r = (p[:, :D_MODEL] * cq).astype(_BF)
    m = jnp.dot(r, wout_ref[...], preferred_element_type=_F32)
    h = DEEPNORM_ALPHA * x + gate * m
    o_ref[...] = _layer_norm(h, g_ref[...], b_ref[...])


def _conv_mixer_layer(x, mod, w_in, conv_w, conv_b, w_out, ln_g, ln_b):
    bsz, seq_len, _ = x.shape
    tl = min(ROW_TILE, seq_len)
    main, prev, nxt = _row_specs(seq_len, tl)
    return pl.pallas_call(
        _conv_mixer_kernel,
        grid=(bsz, seq_len // tl),
        in_specs=[main, prev, nxt, _mod_spec(),
                  _const_spec((D_MODEL, 3 * D_MODEL)), _const_spec((3, D_MODEL)),
                  _const_spec((1, D_MODEL)), _const_spec((D_MODEL, D_MODEL)),
                  _const_spec((1, D_MODEL)), _const_spec((1, D_MODEL))],
        out_specs=main,
        out_shape=jax.ShapeDtypeStruct(x.shape, _F32),
        scratch_shapes=[pltpu.VMEM((tl + 2 * SUBLANES, D_MODEL), _F32)],
        compiler_params=pltpu.CompilerParams(
            dimension_semantics=("parallel", "parallel"), vmem_limit_bytes=VMEM_LIMIT),
        name="conv_mixer_layer",
    )(x, x, x, mod, w_in.astype(_BF), conv_w, conv_b.reshape(1, D_MODEL), w_out.astype(_BF),
      ln_g.reshape(1, D_MODEL), ln_b.reshape(1, D_MODEL))


def _ffn_kernel(emit_next, x_ref, xp_ref, xn_ref, mod_ref, *rest):
    if emit_next:
        modn_ref, rest = rest[0], rest[1:]
    wup_ref, cw_ref, cb_ref, wdn_ref, g_ref, b_ref = rest[:6]
    o_ref = rest[6]
    if emit_next:
        un_ref, rest = rest[7], rest[8:]
    else:
        rest = rest[7:]
    bufa_ref, bufv_ref = rest
    t = pl.program_id(1)
    first = t == 0
    last = t == pl.num_programs(1) - 1
    sh = mod_ref[3:4, :]
    sc = 1.0 + mod_ref[4:5, :]
    gate = 1.0 + mod_ref[5:6, :]
    x = x_ref[...]
    u = (x * sc + sh).astype(_BF)
    uh = jnp.concatenate([xp_ref[...], xn_ref[...]], axis=0)
    uh = (uh * sc + sh).astype(_BF)
    acc = None
    k0 = 0
    for kc in FFN_COL_CHUNKS:
        halves = []
        for base, buf_ref in ((0, bufa_ref), (FFN_HIDDEN, bufv_ref)):
            lo = base + k0
            w = wup_ref[:, lo:lo + kc]
            a = jnp.dot(u, w, preferred_element_type=_F32)
            ah = jnp.dot(uh, w, preferred_element_type=_F32)
            halves.append(_conv3_rows(
                buf_ref.at[:, 0:kc], a, ah[SUBLANES - 1:SUBLANES, :], ah[SUBLANES:SUBLANES + 1, :],
                cw_ref[:, lo:lo + kc], cb_ref[:, lo:lo + kc], first, last))
        hid = (_gelu(halves[0]) * halves[1]).astype(_BF)
        part = jnp.dot(hid, wdn_ref[k0:k0 + kc, :], preferred_element_type=_F32)
        acc = part if acc is None else acc + part
        k0 += kc
    h = DEEPNORM_ALPHA * x + gate * acc
    y = _layer_norm(h, g_ref[...], b_ref[...])
    o_ref[...] = y
    if emit_next:
        un_ref[...] = (y * (1.0 + modn_ref[1:2, :]) + modn_ref[0:1, :]).astype(_BF)


def _ffn_layer(x, mod, w_up, conv_w, conv_b, w_down, ln_g, ln_b, mod_next=None):
    bsz, seq_len, _ = x.shape
    tl = min(ROW_TILE, seq_len)
    emit_next = mod_next is not None
    main, prev, nxt = _row_specs(seq_len, tl)
    in_specs = [main, prev, nxt, _mod_spec()]
    args = [x, x, x, mod]
    if emit_next:
        in_specs.append(_mod_spec())
        args.append(mod_next)
    in_specs += [_const_spec((D_MODEL, 2 * FFN_HIDDEN)), _const_spec((3, 2 * FFN_HIDDEN)),
                 _const_spec((1, 2 * FFN_HIDDEN)), _const_spec((FFN_HIDDEN, D_MODEL)),
                 _const_spec((1, D_MODEL)), _const_spec((1, D_MODEL))]
    args += [w_up.astype(_BF), conv_w, conv_b.reshape(1, 2 * FFN_HIDDEN), w_down.astype(_BF),
             ln_g.reshape(1, D_MODEL), ln_b.reshape(1, D_MODEL)]
    out_specs = [main]
    out_shape = [jax.ShapeDtypeStruct(x.shape, _F32)]
    if emit_next:
        out_specs.append(main)
        out_shape.append(jax.ShapeDtypeStruct(x.shape, _BF))
    kmax = max(FFN_COL_CHUNKS)
    res = pl.pallas_call(
        functools.partial(_ffn_kernel, emit_next),
        grid=(bsz, seq_len // tl),
        in_specs=in_specs,
        out_specs=out_specs,
        out_shape=out_shape,
        scratch_shapes=[pltpu.VMEM((tl + 2 * SUBLANES, kmax), _F32),
                        pltpu.VMEM((tl + 2 * SUBLANES, kmax), _F32)],
        compiler_params=pltpu.CompilerParams(
            dimension_semantics=("parallel", "parallel"), vmem_limit_bytes=VMEM_LIMIT),
        name="conv_ffn_layer",
    )(*args)
    return (res[0], res[1]) if emit_next else (res[0], None)


def _s5_tables(a_re, a_im, log_dt, b_re, b_im, c_re, c_im):
    tt = CHUNK
    dt = jnp.exp(log_dt)[..., None]
    lr = a_re * dt
    li = a_im * dt
    n = jnp.arange(tt + 1, dtype=_F32)[:, None, None, None]
    mag = jnp.exp(lr[None] * n)
    pw_re = mag * jnp.cos(li[None] * n)
    pw_im = mag * jnp.sin(li[None] * n)
    lb_re, lb_im = pw_re[1], pw_im[1]
    den = a_re * a_re + a_im * a_im
    f_re = ((lb_re - 1.0) * a_re + lb_im * a_im) / den
    f_im = (lb_im * a_re - (lb_re - 1.0) * a_im) / den
    w_re = pw_re * f_re[None] - pw_im * f_im[None]
    w_im = pw_re * f_im[None] + pw_im * f_re[None]

    def kern(k):
        wr = w_re[:tt, k][:, :, None, :]
        wi = w_im[:tt, k][:, :, None, :]
        xr = c_re[k][None] * wr - c_im[k][None] * wi
        xi = c_re[k][None] * wi + c_im[k][None] * wr
        return (jnp.einsum('ngip,gpj->ngij', xr, b_re[k], precision=_HI)
                - jnp.einsum('ngip,gpj->ngij', xi, b_im[k], precision=_HI))

    kf, kb = kern(0), kern(1)
    tau = jnp.arange(tt)[:, None]
    tq = jnp.arange(tt)[None, :]
    mf = jnp.where((tq >= tau)[:, :, None, None, None], kf[jnp.clip(tq - tau, 0, tt - 1)], 0.0)
    mb = jnp.where((tau >= tq)[:, :, None, None, None], kb[jnp.clip(tau - tq, 0, tt - 1)], 0.0)
    m = (mf + mb).transpose(2, 0, 4, 1, 3)
    m = m.reshape(SSM_GROUPS, CHUNK_COLS, CHUNK_COLS)

    def bt(k, w_r, w_i):
        br = b_re[k].transpose(0, 2, 1)[:, None]
        bi = b_im[k].transpose(0, 2, 1)[:, None]
        wr = w_r.transpose(1, 0, 2)[:, :, None, :]
        wi = w_i.transpose(1, 0, 2)[:, :, None, :]
        re = (wr * br - wi * bi).reshape(SSM_GROUPS, CHUNK_COLS, SSM_STATE)
        im = (wr * bi + wi * br).reshape(SSM_GROUPS, CHUNK_COLS, SSM_STATE)
        return re, im

    bf_re, bf_im = bt(0, w_re[:tt, 0][::-1], w_im[:tt, 0][::-1])
    bb_re, bb_im = bt(1, w_re[:tt, 1], w_im[:tt, 1])
    bt_all = jnp.concatenate([bf_re, bb_re, bf_im, bb_im], axis=-1)

    def ct(k, e_r, e_i):
        cr = c_re[k].transpose(0, 2, 1)[:, :, None, :]
        ci = c_im[k].transpose(0, 2, 1)[:, :, None, :]
        er = e_r.transpose(1, 2, 0)[:, :, :, None]
        ei = e_i.transpose(1, 2, 0)[:, :, :, None]
        on_re = (cr * er - ci * ei).reshape(SSM_GROUPS, SSM_STATE, CHUNK_COLS)
        on_im = (-(cr * ei + ci * er)).reshape(SSM_GROUPS, SSM_STATE, CHUNK_COLS)
        return on_re, on_im

    cf_re, cf_im = ct(0, pw_re[1:, 0], pw_im[1:, 0])
    cb_re, cb_im = ct(1, pw_re[1:, 1][::-1], pw_im[1:, 1][::-1])
    ct_all = jnp.concatenate([cf_re, cb_re, cf_im, cb_im], axis=1)

    decay = jnp.stack([jnp.concatenate([pw_re[tt, 0], pw_re[tt, 1]], axis=-1),
                       jnp.concatenate([pw_im[tt, 0], pw_im[tt, 1]], axis=-1)], axis=1)
    return m.astype(_BF), bt_all.astype(_BF), ct_all.astype(_BF), decay


def _s5_chunk_kernel(bsz, n_chunks, u_ref, m_ref, bt_ref, ct_ref, dec_ref, y_ref,
                     loc_ref, carry_ref):
    half = SSM_STATE
    u = u_ref[...]
    loc_ref[...] = jnp.dot(u, bt_ref[...], preferred_element_type=_F32)
    a_re = jnp.broadcast_to(dec_ref[0:1, :], (bsz, 2 * half))
    a_im = jnp.broadcast_to(dec_ref[1:2, :], (bsz, 2 * half))
    is_fwd = lax.broadcasted_iota(jnp.int32, (bsz, 2 * half), 1) < half

    def step(k, state):
        s_re, s_im = state
        rf = pl.multiple_of(k * bsz, SUBLANES)
        rb = pl.multiple_of((n_chunks - 1 - k) * bsz, SUBLANES)
        carry_ref[pl.ds(rf, bsz), 0:half] = s_re[:, 0:half]
        carry_ref[pl.ds(rb, bsz), half:2 * half] = s_re[:, half:]
        carry_ref[pl.ds(rf, bsz), 2 * half:3 * half] = s_im[:, 0:half]
        carry_ref[pl.ds(rb, bsz), 3 * half:] = s_im[:, half:]
        l_re = jnp.where(is_fwd, loc_ref[pl.ds(rf, bsz), 0:2 * half],
                         loc_ref[pl.ds(rb, bsz), 0:2 * half])
        l_im = jnp.where(is_fwd, loc_ref[pl.ds(rf, bsz), 2 * half:],
                         loc_ref[pl.ds(rb, bsz), 2 * half:])
        n_re = a_re * s_re - a_im * s_im + l_re
        n_im = a_re * s_im + a_im * s_re + l_im
        return n_re, n_im

    zero = jnp.zeros((bsz, 2 * half), _F32)
    lax.fori_loop(0, n_chunks, step, (zero, zero))
    y = jnp.dot(u, m_ref[...], preferred_element_type=_F32)
    y = y + jnp.dot(carry_ref[...].astype(_BF), ct_ref[...], preferred_element_type=_F32)
    y_ref[...] = y


def _s5_chunk_call(ur, m, bt, ct, decay, bsz, n_chunks):
    rows = n_chunks * bsz
    grp = lambda shape: pl.BlockSpec((None,) + shape, lambda g: (g, 0, 0))
    return pl.pallas_call(
        functools.partial(_s5_chunk_kernel, bsz, n_chunks),
        grid=(SSM_GROUPS,),
        in_specs=[grp((rows, CHUNK_COLS)), grp((CHUNK_COLS, CHUNK_COLS)),
                  grp((CHUNK_COLS, CHUNK_COLS)), grp((CHUNK_COLS, CHUNK_COLS)),
                  grp((2, 2 * SSM_STATE))],
        out_specs=grp((rows, CHUNK_COLS)),
        out_shape=jax.ShapeDtypeStruct((SSM_GROUPS, rows, CHUNK_COLS), _F32),
        scratch_shapes=[pltpu.VMEM((rows, CHUNK_COLS), _F32),
                        pltpu.VMEM((rows, CHUNK_COLS), _F32)],
        compiler_params=pltpu.CompilerParams(
            dimension_semantics=("parallel",), vmem_limit_bytes=VMEM_LIMIT),
        name="s5_chunk_scan",
    )(ur, m, bt, ct, decay)


def _s5_post_kernel(x_ref, yr_ref, mod_ref, d_ref, wglu_ref, g_ref, b_ref, o_ref):
    sh = mod_ref[0:1, :]
    sc = 1.0 + mod_ref[1:2, :]
    gate = 1.0 + mod_ref[2:3, :]
    x = x_ref[...]
    u = x * sc + sh
    y = d_ref[...] * u + yr_ref[...]
    z = _gelu(y).astype(_BF)
    pz = jnp.dot(z, wglu_ref[...], preferred_element_type=_F32)
    m = pz[:, :D_MODEL] * jax.nn.sigmoid(pz[:, D_MODEL:])
    h = DEEPNORM_ALPHA * x + gate * m
    o_ref[...] = _layer_norm(h, g_ref[...], b_ref[...])


def _s5_mixer_layer(x, u_bf, mod, tables, d_skip, w_glu, ln_g, ln_b):
    bsz, seq_len, _ = x.shape
    n_chunks = seq_len // CHUNK
    ur = u_bf.reshape(bsz, n_chunks, CHUNK, SSM_GROUPS, SSM_GROUP)
    ur = ur.transpose(3, 1, 0, 2, 4).reshape(SSM_GROUPS, n_chunks * bsz, CHUNK_COLS)
    yr = _s5_chunk_call(ur, *tables, bsz, n_chunks)
    yr = yr.reshape(SSM_GROUPS, n_chunks, bsz, CHUNK, SSM_GROUP)
    yr = yr.transpose(2, 1, 3, 0, 4).reshape(bsz, seq_len, D_MODEL)
    tl = min(ROW_TILE, seq_len)
    main = pl.BlockSpec((None, tl, D_MODEL), lambda b, t: (b, t, 0))
    return pl.pallas_call(
        _s5_post_kernel,
        grid=(bsz, seq_len // tl),
        in_specs=[main, main, _mod_spec(), _const_spec((1, D_MODEL)),
                  _const_spec((D_MODEL, 2 * D_MODEL)),
                  _const_spec((1, D_MODEL)), _const_spec((1, D_MODEL))],
        out_specs=main,
        out_shape=jax.ShapeDtypeStruct(x.shape, _F32),
        compiler_params=pltpu.CompilerParams(
            dimension_semantics=("parallel", "parallel"), vmem_limit_bytes=VMEM_LIMIT),
        name="s5_glu_layer",
    )(x, yr, mod, d_skip.reshape(1, D_MODEL), w_glu.astype(_BF),
      ln_g.reshape(1, D_MODEL), ln_b.reshape(1, D_MODEL))


def _trunk(x, c, p, s5_tables):
    mod = _ada(c, p["ada_w"], p["ada_b"])
    u_next = None
    for i in range(DEPTH):
        j = i // 2
        if i % 2 == 0:
            x = _conv_mixer_layer(x, mod[i], p["sc_w_in"][j], p["sc_conv_w"][j],
                                  p["sc_conv_b"][j], p["sc_w_out"][j], p["ln1_g"][i], p["ln1_b"][i])
        else:
            x = _s5_mixer_layer(x, u_next, mod[i], s5_tables[j], p["s5_d"][j], p["s5_w_glu"][j],
                                p["ln1_g"][i], p["ln1_b"][i])
        nxt_is_s5 = i + 1 < DEPTH and (i + 1) % 2 == 1
        x, u_next = _ffn_layer(x, mod[i], p["ffn_w_up"][i], p["ffn_conv_w"][i], p["ffn_conv_b"][i],
                               p["ffn_w_down"][i], p["ln2_g"][i], p["ln2_b"][i],
                               mod_next=mod[i + 1] if nxt_is_s5 else None)
    return x


def kernel(x_prompt, x_sample, c_prompt, c_sample, ada_w, ada_b, ln1_g, ln1_b, ln2_g, ln2_b, sc_w_in, sc_conv_w, sc_conv_b, sc_w_out, s5_a_re, s5_a_im, s5_log_dt, s5_b_re, s5_b_im, s5_c_re, s5_c_im, s5_d, s5_w_glu, ffn_w_up, ffn_conv_w, ffn_conv_b, ffn_w_down):
    p = dict(ada_w=ada_w, ada_b=ada_b, ln1_g=ln1_g, ln1_b=ln1_b, ln2_g=ln2_g, ln2_b=ln2_b,
             sc_w_in=sc_w_in, sc_conv_w=sc_conv_w, sc_conv_b=sc_conv_b, sc_w_out=sc_w_out,
             s5_d=s5_d, s5_w_glu=s5_w_glu, ffn_w_up=ffn_w_up, ffn_conv_w=ffn_conv_w,
             ffn_conv_b=ffn_conv_b, ffn_w_down=ffn_w_down)
    s5_tables = [_s5_tables(s5_a_re[j], s5_a_im[j], s5_log_dt[j], s5_b_re[j], s5_b_im[j],
                            s5_c_re[j], s5_c_im[j]) for j in range(s5_a_re.shape[0])]
    y_prompt = _trunk(x_prompt, c_prompt, p, s5_tables)
    y_sample = _trunk(x_sample, c_sample, p, s5_tables)
    return (y_prompt, y_sample)
```

```python
import functools
import math

import jax
import jax.numpy as jnp
from jax import lax
from jax.experimental import pallas as pl
from jax.experimental.pallas import tpu as pltpu

D_MODEL = 1024
DEPTH = 4
FFN_HIDDEN = 2816
SSM_GROUP = 16
SSM_GROUPS = D_MODEL // SSM_GROUP
SSM_STATE = 64
DEEPNORM_ALPHA = (2 * DEPTH) ** 0.25
LN_EPS = 1e-5

SUBLANES = 8
LANES = 128
CHUNK = 16
CHUNK_COLS = CHUNK * SSM_GROUP
TOK_TILE = 64
GLU_TOKS = 4
FFN_COL_CHUNKS = (768, 768, 768, 512)
VMEM_LIMIT = 56 * 1024 * 1024

_HI = lax.Precision.HIGHEST
_BF = jnp.bfloat16
_F32 = jnp.float32


def _const_spec(shape):
    nd = len(shape)
    return pl.BlockSpec(shape, lambda *_: (0,) * nd, pipeline_mode=pl.Buffered(1))


def _params(*sem):
    return pltpu.CompilerParams(dimension_semantics=sem, vmem_limit_bytes=VMEM_LIMIT)


def _gelu(x):
    c = math.sqrt(2.0 / math.pi)
    return 0.5 * x * (1.0 + jnp.tanh(c * (x + 0.044715 * (x * x * x))))


def _layer_norm(h, g, b):
    mu = jnp.mean(h, axis=-1, keepdims=True)
    d = h - mu
    var = jnp.mean(d * d, axis=-1, keepdims=True)
    return d * lax.rsqrt(var + LN_EPS) * g + b


def _conv3_rows(cur, before, after, w, b):
    up = jnp.concatenate([before, cur[:-SUBLANES]], axis=0)
    dn = jnp.concatenate([cur[SUBLANES:], after], axis=0)
    return up * w[0:1, :] + cur * w[1:2, :] + dn * w[2:3, :] + b


def _ada_kernel(c_ref, w_ref, b_ref, o_ref):
    c = c_ref[...]
    ca = c * jax.nn.sigmoid(c)
    o_ref[...] = jnp.dot(ca, w_ref[...], precision=_HI, preferred_element_type=_F32) + b_ref[...]


def _ada(c, ada_w, ada_b):
    bsz = c.shape[0]
    return pl.pallas_call(
        _ada_kernel,
        grid=(DEPTH, 6),
        in_specs=[
            pl.BlockSpec((bsz, D_MODEL), lambda i, n: (0, 0)),
            pl.BlockSpec((None, D_MODEL, D_MODEL), lambda i, n: (i, 0, n)),
            pl.BlockSpec((None, 1, D_MODEL), lambda i, n: (i, 0, n)),
        ],
        out_specs=pl.BlockSpec((None, None, bsz, D_MODEL), lambda i, n: (i, n, 0, 0)),
        out_shape=jax.ShapeDtypeStruct((DEPTH, 6, bsz, D_MODEL), _F32),
        compiler_params=_params("arbitrary", "arbitrary"),
        name="ada_mod",
    )(c, ada_w, ada_b.reshape(DEPTH, 1, 6 * D_MODEL))


def _tile_specs(seq_len):
    main = pl.BlockSpec((TOK_TILE, SUBLANES, D_MODEL), lambda b, t: (t, b, 0))
    prev = pl.BlockSpec((1, SUBLANES, D_MODEL),
                        lambda b, t: (jnp.maximum(t * TOK_TILE - 1, 0), b, 0))
    nxt = pl.BlockSpec((1, SUBLANES, D_MODEL),
                       lambda b, t: (jnp.minimum((t + 1) * TOK_TILE, seq_len - 1), b, 0))
    return main, prev, nxt


def _mod_spec():
    return pl.BlockSpec((6, SUBLANES, D_MODEL), lambda b, t: (0, b, 0))


def _edge_rows(xp_ref, xn_ref, sc, sh):
    return jnp.concatenate([xp_ref[0] * sc + sh, xn_ref[0] * sc + sh], axis=0).astype(_BF)


def _conv_mixer_kernel(x_ref, xp_ref, xn_ref, mod_ref, win_ref, cw_ref, cb_ref, wout_ref,
                       g_ref, b_ref, o_ref):
    t = pl.program_id(1)
    first = t == 0
    last = t == pl.num_programs(1) - 1
    sh = mod_ref[0]
    sc = 1.0 + mod_ref[1]
    gate = 1.0 + mod_ref[2]
    x = x_ref[...]
    rows = x.shape[0] * SUBLANES
    u = (x * sc[None] + sh[None]).reshape(rows, D_MODEL).astype(_BF)
    p = jnp.dot(u, win_ref[...], preferred_element_type=_F32)
    q = p[:, D_MODEL:2 * D_MODEL] * p[:, 2 * D_MODEL:]
    ph = jnp.dot(_edge_rows(xp_ref, xn_ref, sc, sh), win_ref[:, D_MODEL:],
                 preferred_element_type=_F32)
    qh = ph[:, :D_MODEL] * ph[:, D_MODEL:]
    zero = jnp.zeros((SUBLANES, D_MODEL), _F32)
    cq = _conv3_rows(q, jnp.where(first, zero, qh[:SUBLANES]), jnp.where(last, zero, qh[SUBLANES:]),
                     cw_ref[...], cb_ref[...])
    r = (p[:, :D_MODEL] * cq).astype(_BF)
    m = jnp.dot(r, wout_ref[...], preferred_element_type=_F32).reshape(x.shape)
    h = DEEPNORM_ALPHA * x + gate[None] * m
    o_ref[...] = _layer_norm(h, g_ref[...], b_ref[...])


def _conv_mixer_layer(xt, mod, w_in, conv_w, conv_b, w_out, ln_g, ln_b):
    seq_len, bsz, _ = xt.shape
    main, prev, nxt = _tile_specs(seq_len)
    return pl.pallas_call(
        _conv_mixer_kernel,
        grid=(bsz // SUBLANES, seq_len // TOK_TILE),
        in_specs=[main, prev, nxt, _mod_spec(),
                  _const_spec((D_MODEL, 3 * D_MODEL)), _const_spec((3, D_MODEL)),
                  _const_spec((1, D_MODEL)), _const_spec((D_MODEL, D_MODEL)),
                  _const_spec((1, D_MODEL)), _const_spec((1, D_MODEL))],
        out_specs=main,
        out_shape=jax.ShapeDtypeStruct(xt.shape, _F32),
        compiler_params=_params("parallel", "parallel"),
        name="conv_mixer_layer",
    )(xt, xt, xt, mod, w_in.astype(_BF), conv_w, conv_b.reshape(1, D_MODEL), w_out.astype(_BF),
      ln_g.reshape(1, D_MODEL), ln_b.reshape(1, D_MODEL))


def _ffn_kernel(x_ref, xp_ref, xn_ref, mod_ref, wup_ref, cw_ref, cb_ref, wdn_ref, g_ref, b_ref,
                o_ref):
    t = pl.program_id(1)
    first = t == 0
    last = t == pl.num_programs(1) - 1
    sh = mod_ref[3]
    sc = 1.0 + mod_ref[4]
    gate = 1.0 + mod_ref[5]
    x = x_ref[...]
    rows = x.shape[0] * SUBLANES
    u = (x * sc[None] + sh[None]).reshape(rows, D_MODEL).astype(_BF)
    uh = _edge_rows(xp_ref, xn_ref, sc, sh)
    acc = None
    k0 = 0
    for kc in FFN_COL_CHUNKS:
        zero = jnp.zeros((SUBLANES, kc), _F32)
        halves = []
        for base in (0, FFN_HIDDEN):
            lo = base + k0
            w = wup_ref[:, lo:lo + kc]
            a = jnp.dot(u, w, preferred_element_type=_F32)
            ah = jnp.dot(uh, w, preferred_element_type=_F32)
            halves.append(_conv3_rows(
                a, jnp.where(first, zero, ah[:SUBLANES]), jnp.where(last, zero, ah[SUBLANES:]),
                cw_ref[:, lo:lo + kc], cb_ref[:, lo:lo + kc]))
        hid = (_gelu(halves[0]) * halves[1]).astype(_BF)
        part = jnp.dot(hid, wdn_ref[k0:k0 + kc, :], preferred_element_type=_F32)
        acc = part if acc is None else acc + part
        k0 += kc
    h = DEEPNORM_ALPHA * x + gate[None] * acc.reshape(x.shape)
    o_ref[...] = _layer_norm(h, g_ref[...], b_ref[...])


def _ffn_layer(xt, mod, w_up, conv_w, conv_b, w_down, ln_g, ln_b):
    seq_len, bsz, _ = xt.shape
    main, prev, nxt = _tile_specs(seq_len)
    return pl.pallas_call(
        _ffn_kernel,
        grid=(bsz // SUBLANES, seq_len // TOK_TILE),
        in_specs=[main, prev, nxt, _mod_spec(),
                  _const_spec((D_MODEL, 2 * FFN_HIDDEN)), _const_spec((3, 2 * FFN_HIDDEN)),
                  _const_spec((1, 2 * FFN_HIDDEN)), _const_spec((FFN_HIDDEN, D_MODEL)),
                  _const_spec((1, D_MODEL)), _const_spec((1, D_MODEL))],
        out_specs=main,
        out_shape=jax.ShapeDtypeStruct(xt.shape, _F32),
        compiler_params=_params("parallel", "parallel"),
        name="conv_ffn_layer",
    )(xt, xt, xt, mod, w_up.astype(_BF), conv_w, conv_b.reshape(1, 2 * FFN_HIDDEN),
      w_down.astype(_BF), ln_g.reshape(1, D_MODEL), ln_b.reshape(1, D_MODEL))


def _s5_tables(a_re, a_im, log_dt, b_re, b_im, c_re, c_im):
    tt = CHUNK
    dt = jnp.exp(log_dt)[..., None]
    lr = a_re * dt
    li = a_im * dt
    n = jnp.arange(tt + 1, dtype=_F32)[:, None, None, None]
    mag = jnp.exp(lr[None] * n)
    pw_re = mag * jnp.cos(li[None] * n)
    pw_im = mag * jnp.sin(li[None] * n)
    lb_re, lb_im = pw_re[1], pw_im[1]
    den = a_re * a_re + a_im * a_im
    f_re = ((lb_re - 1.0) * a_re + lb_im * a_im) / den
    f_im = (lb_im * a_re - (lb_re - 1.0) * a_im) / den
    w_re = pw_re * f_re[None] - pw_im * f_im[None]
    w_im = pw_re * f_im[None] + pw_im * f_re[None]

    def kern(k):
        wr = w_re[:tt, k][:, :, None, :]
        wi = w_im[:tt, k][:, :, None, :]
        xr = c_re[k][None] * wr - c_im[k][None] * wi
        xi = c_re[k][None] * wi + c_im[k][None] * wr
        return (jnp.einsum('ngip,gpj->ngij', xr, b_re[k], precision=_HI)
                - jnp.einsum('ngip,gpj->ngij', xi, b_im[k], precision=_HI))

    kf, kb = kern(0), kern(1)
    tau = jnp.arange(tt)[:, None]
    tq = jnp.arange(tt)[None, :]
    mf = jnp.where((tq >= tau)[:, :, None, None, None], kf[jnp.clip(tq - tau, 0, tt - 1)], 0.0)
    mb = jnp.where((tau >= tq)[:, :, None, None, None], kb[jnp.clip(tau - tq, 0, tt - 1)], 0.0)
    m_t = (mf + mb).transpose(2, 1, 3, 0, 4)
    m_t = m_t.reshape(SSM_GROUPS, CHUNK_COLS, CHUNK_COLS)

    def bt(k, w_r, w_i):
        br = b_re[k].transpose(0, 2, 1)[:, None]
        bi = b_im[k].transpose(0, 2, 1)[:, None]
        wr = w_r.transpose(1, 0, 2)[:, :, None, :]
        wi = w_i.transpose(1, 0, 2)[:, :, None, :]
        re = (wr * br - wi * bi).reshape(SSM_GROUPS, CHUNK_COLS, SSM_STATE)
        im = (wr * bi + wi * br).reshape(SSM_GROUPS, CHUNK_COLS, SSM_STATE)
        return re, im

    bf_re, bf_im = bt(0, w_re[:tt, 0][::-1], w_im[:tt, 0][::-1])
    bb_re, bb_im = bt(1, w_re[:tt, 1], w_im[:tt, 1])
    bt_all = jnp.concatenate([bf_re, bb_re, bf_im, bb_im], axis=-1)

    def ct_t(k, e_r, e_i):
        cr = c_re[k][:, None]
        ci = c_im[k][:, None]
        er = e_r.transpose(1, 0, 2)[:, :, None, :]
        ei = e_i.transpose(1, 0, 2)[:, :, None, :]
        on_re = (cr * er - ci * ei).reshape(SSM_GROUPS, CHUNK_COLS, SSM_STATE)
        on_im = (-(cr * ei + ci * er)).reshape(SSM_GROUPS, CHUNK_COLS, SSM_STATE)
        return on_re, on_im

    cf_re, cf_im = ct_t(0, pw_re[1:, 0], pw_im[1:, 0])
    cb_re, cb_im = ct_t(1, pw_re[1:, 1][::-1], pw_im[1:, 1][::-1])
    ct_all = jnp.concatenate([cf_re, cb_re, cf_im, cb_im], axis=-1)

    decay = jnp.stack([jnp.concatenate([pw_re[tt, 0], pw_re[tt, 1]], axis=-1),
                       jnp.concatenate([pw_im[tt, 0], pw_im[tt, 1]], axis=-1)], axis=1)
    return m_t.astype(_BF), bt_all.astype(_BF), ct_all.astype(_BF), decay


def _s5_regroup_kernel(x_ref, mod_ref, o_ref):
    sh = mod_ref[0]
    sc = 1.0 + mod_ref[1]
    cols = x_ref.shape[0] * x_ref.shape[2]
    for tau in range(CHUNK):
        slab = (x_ref[:, tau] * sc[None] + sh[None]).reshape(cols, D_MODEL)
        o_ref[:, SSM_GROUP * tau:SSM_GROUP * (tau + 1), :] = (
            slab.T.astype(_BF).reshape(SSM_GROUPS, SSM_GROUP, cols))


def _s5_chunk_kernel(bsz, n_chunks, z_ref, mt_ref, bt_ref, ct_ref, dec_ref, y_ref,
                     loc_ref, carry_ref):
    half = SSM_STATE
    z = z_ref[...]
    loc_ref[...] = lax.dot_general(z, bt_ref[...], (((0,), (0,)), ((), ())),
                                   preferred_element_type=_F32)
    a_re = jnp.broadcast_to(dec_ref[0:1, :], (bsz, 2 * half))
    a_im = jnp.broadcast_to(dec_ref[1:2, :], (bsz, 2 * half))
    is_fwd = lax.broadcasted_iota(jnp.int32, (bsz, 2 * half), 1) < half

    def step(k, state):
        s_re, s_im = state
        rf = pl.multiple_of(k * bsz, SUBLANES)
        rb = pl.multiple_of((n_chunks - 1 - k) * bsz, SUBLANES)
        carry_ref[pl.ds(rf, bsz), 0:half] = s_re[:, 0:half]
        carry_ref[pl.ds(rb, bsz), half:2 * half] = s_re[:, half:]
        carry_ref[pl.ds(rf, bsz), 2 * half:3 * half] = s_im[:, 0:half]
        carry_ref[pl.ds(rb, bsz), 3 * half:] = s_im[:, half:]
        l_re = jnp.where(is_fwd, loc_ref[pl.ds(rf, bsz), 0:2 * half],
                         loc_ref[pl.ds(rb, bsz), 0:2 * half])
        l_im = jnp.where(is_fwd, loc_ref[pl.ds(rf, bsz), 2 * half:],
                         loc_ref[pl.ds(rb, bsz), 2 * half:])
        n_re = a_re * s_re - a_im * s_im + l_re
        n_im = a_re * s_im + a_im * s_re + l_im
        return n_re, n_im

    zero = jnp.zeros((bsz, 2 * half), _F32)
    lax.fori_loop(0, n_chunks, step, (zero, zero))
    y = jnp.dot(mt_ref[...], z, preferred_element_type=_F32)
    y = y + lax.dot_general(ct_ref[...], carry_ref[...].astype(_BF), (((1,), (1,)), ((), ())),
                            preferred_element_type=_F32)
    y_ref[...] = y.astype(_BF)


def _s5_glu_kernel(x_ref, yt_ref, mod_ref, d_ref, wglu_ref, g_ref, b_ref, o_ref):
    sh = mod_ref[0]
    sc = 1.0 + mod_ref[1]
    gate = 1.0 + mod_ref[2]
    ncb, ntok, bsz, _ = x_ref.shape
    cols = ncb * bsz
    zs = []
    for tq in range(ntok):
        yd = yt_ref[:, SSM_GROUP * tq:SSM_GROUP * (tq + 1), :].astype(_F32)
        yd = yd.reshape(D_MODEL, cols).T.reshape(ncb, bsz, D_MODEL)
        u = x_ref[:, tq] * sc[None] + sh[None]
        zs.append(_gelu(d_ref[...] * u + yd).reshape(cols, D_MODEL).astype(_BF))
    pz = jnp.dot(jnp.concatenate(zs, axis=0), wglu_ref[...], preferred_element_type=_F32)
    m = pz[:, :D_MODEL] * jax.nn.sigmoid(pz[:, D_MODEL:])
    for tq in range(ntok):
        mt = m[cols * tq:cols * (tq + 1)].reshape(ncb, bsz, D_MODEL)
        h = DEEPNORM_ALPHA * x_ref[:, tq] + gate[None] * mt
        o_ref[:, tq] = _layer_norm(h, g_ref[...], b_ref[...])


def _s5_mixer_layer(xt, mod, tables, d_skip, w_glu, ln_g, ln_b):
    seq_len, bsz, _ = xt.shape
    n_chunks = seq_len // CHUNK
    ncb = LANES // bsz
    cols = n_chunks * bsz
    x4 = xt.reshape(n_chunks, CHUNK, bsz, D_MODEL)
    mod_full = pl.BlockSpec((6, bsz, D_MODEL), lambda *_: (0, 0, 0))
    zt = pl.pallas_call(
        _s5_regroup_kernel,
        grid=(n_chunks // ncb,),
        in_specs=[pl.BlockSpec((ncb, CHUNK, bsz, D_MODEL), lambda i: (i, 0, 0, 0)), mod_full],
        out_specs=pl.BlockSpec((SSM_GROUPS, CHUNK_COLS, LANES), lambda i: (0, 0, i)),
        out_shape=jax.ShapeDtypeStruct((SSM_GROUPS, CHUNK_COLS, cols), _BF),
        compiler_params=_params("parallel"),
        name="s5_regroup",
    )(x4, mod)
    grp = lambda r, c: pl.BlockSpec((None, r, c), lambda g: (g, 0, 0))
    m_t, bt, ct_t, decay = tables
    yt = pl.pallas_call(
        functools.partial(_s5_chunk_kernel, bsz, n_chunks),
        grid=(SSM_GROUPS,),
        in_specs=[grp(CHUNK_COLS, cols), grp(CHUNK_COLS, CHUNK_COLS), grp(CHUNK_COLS, CHUNK_COLS),
                  grp(CHUNK_COLS, CHUNK_COLS), grp(2, 2 * SSM_STATE)],
        out_specs=grp(CHUNK_COLS, cols),
        out_shape=jax.ShapeDtypeStruct((SSM_GROUPS, CHUNK_COLS, cols), _BF),
        scratch_shapes=[pltpu.VMEM((cols, CHUNK_COLS), _F32), pltpu.VMEM((cols, CHUNK_COLS), _F32)],
        compiler_params=_params("parallel"),
        name="s5_chunk_scan",
    )(zt, m_t, bt, ct_t, decay)
    xblk = pl.BlockSpec((ncb, GLU_TOKS, bsz, D_MODEL), lambda i, q: (i, q, 0, 0))
    out = pl.pallas_call(
        _s5_glu_kernel,
        grid=(n_chunks // ncb, CHUNK // GLU_TOKS),
        in_specs=[xblk,
                  pl.BlockSpec((SSM_GROUPS, SSM_GROUP * GLU_TOKS, LANES), lambda i, q: (0, q, i)),
                  mod_full, _const_spec((1, D_MODEL)), _const_spec((D_MODEL, 2 * D_MODEL)),
                  _const_spec((1, D_MODEL)), _const_spec((1, D_MODEL))],
        out_specs=xblk,
        out_shape=jax.ShapeDtypeStruct(x4.shape, _F32),
        compiler_params=_params("parallel", "parallel"),
        name="s5_glu_layer",
    )(x4, yt, mod, d_skip.reshape(1, D_MODEL), w_glu.astype(_BF),
      ln_g.reshape(1, D_MODEL), ln_b.reshape(1, D_MODEL))
    return out.reshape(xt.shape)


def _trunk(x, c, p, s5_tables):
    mod = _ada(c, p["ada_w"], p["ada_b"])
    xt = x.transpose(1, 0, 2)
    for i in range(DEPTH):
        j = i // 2
        if i % 2 == 0:
            xt = _conv_mixer_layer(xt, mod[i], p["sc_w_in"][j], p["sc_conv_w"][j],
                                   p["sc_conv_b"][j], p["sc_w_out"][j], p["ln1_g"][i], p["ln1_b"][i])
        else:
            xt = _s5_mixer_layer(xt, mod[i], s5_tables[j], p["s5_d"][j], p["s5_w_glu"][j],
                                 p["ln1_g"][i], p["ln1_b"][i])
        xt = _ffn_layer(xt, mod[i], p["ffn_w_up"][i], p["ffn_conv_w"][i], p["ffn_conv_b"][i],
                        p["ffn_w_down"][i], p["ln2_g"][i], p["ln2_b"][i])
    return xt.transpose(1, 0, 2)


def kernel(x_prompt, x_sample, c_prompt, c_sample, ada_w, ada_b, ln1_g, ln1_b, ln2_g, ln2_b, sc_w_in, sc_conv_w, sc_conv_b, sc_w_out, s5_a_re, s5_a_im, s5_log_dt, s5_b_re, s5_b_im, s5_c_re, s5_c_im, s5_d, s5_w_glu, ffn_w_up, ffn_conv_w, ffn_conv_b, ffn_w_down):
    p = dict(ada_w=ada_w, ada_b=ada_b, ln1_g=ln1_g, ln1_b=ln1_b, ln2_g=ln2_g, ln2_b=ln2_b,
             sc_w_in=sc_w_in, sc_conv_w=sc_conv_w, sc_conv_b=sc_conv_b, sc_w_out=sc_w_out,
             s5_d=s5_d, s5_w_glu=s5_w_glu, ffn_w_up=ffn_w_up, ffn_conv_w=ffn_conv_w,
             ffn_conv_b=ffn_conv_b, ffn_w_down=ffn_w_down)
    s5_tables = [_s5_tables(s5_a_re[j], s5_a_im[j], s5_log_dt[j], s5_b_re[j], s5_b_im[j],
                            s5_c_re[j], s5_c_im[j]) for j in range(s5_a_re.shape[0])]
    y_prompt = _trunk(x_prompt, c_prompt, p, s5_tables)
    y_sample = _trunk(x_sample, c_sample, p, s5_tables)
    return (y_prompt, y_sample)
```

```python
import functools
import math

import jax
import jax.numpy as jnp
from jax import lax
from jax.experimental import pallas as pl
from jax.experimental.pallas import tpu as pltpu

D_MODEL = 1024
DEPTH = 4
FFN_HIDDEN = 2816
SSM_GROUP = 16
SSM_GROUPS = D_MODEL // SSM_GROUP
SSM_STATE = 64
DEEPNORM_ALPHA = (2 * DEPTH) ** 0.25
LN_EPS = 1e-5

SUBLANES = 8
LANES = 128
CHUNK = 16
CHUNK_COLS = CHUNK * SSM_GROUP
TOK_TILE = 128
GLU_TOKS = 16
GLU_SUB = 2
S5_GROUPS_PER_STEP = 2
S5_SCAN_UNROLL = 2
FFN_COL_CHUNKS = (1280, 1536)
ROW_SPLITS = 4
VMEM_LIMIT = 56 * 1024 * 1024

_HI = lax.Precision.HIGHEST
_BF = jnp.bfloat16
_F32 = jnp.float32


def _const_spec(shape):
    nd = len(shape)
    return pl.BlockSpec(shape, lambda *_: (0,) * nd, pipeline_mode=pl.Buffered(1))


def _params(*sem):
    return pltpu.CompilerParams(dimension_semantics=sem, vmem_limit_bytes=VMEM_LIMIT)


_GELU_C1 = math.sqrt(2.0 / math.pi)
_GELU_C2 = _GELU_C1 * 0.044715


def _gelu_half_gate(x):
    return 0.5 + 0.5 * jnp.tanh(x * (_GELU_C1 + _GELU_C2 * (x * x)))


def _gelu(x):
    return x * _gelu_half_gate(x)


def _gelu_times(x, v):
    return (x * v) * _gelu_half_gate(x)


def _layer_norm(h, g, b):
    mu = jnp.mean(h, axis=-1, keepdims=True)
    d = h - mu
    var = jnp.mean(d * d, axis=-1, keepdims=True)
    return d * lax.rsqrt(var + LN_EPS) * g + b


def _conv3_rows(cur, before, after, w, b):
    up = jnp.concatenate([before, cur[:-SUBLANES]], axis=0)
    dn = jnp.concatenate([cur[SUBLANES:], after], axis=0)
    return up * w[0:1, :] + cur * w[1:2, :] + dn * w[2:3, :] + b


def _ada_kernel(c_ref, w_ref, b_ref, o_ref):
    c = c_ref[...]
    ca = c * jax.nn.sigmoid(c)
    o_ref[...] = jnp.dot(ca, w_ref[...], precision=_HI, preferred_element_type=_F32) + b_ref[...]


def _ada(c, ada_w, ada_b):
    bsz = c.shape[0]
    return pl.pallas_call(
        _ada_kernel,
        grid=(DEPTH, 6),
        in_specs=[
            pl.BlockSpec((bsz, D_MODEL), lambda i, n: (0, 0)),
            pl.BlockSpec((None, D_MODEL, D_MODEL), lambda i, n: (i, 0, n)),
            pl.BlockSpec((None, 1, D_MODEL), lambda i, n: (i, 0, n)),
        ],
        out_specs=pl.BlockSpec((None, None, bsz, D_MODEL), lambda i, n: (i, n, 0, 0)),
        out_shape=jax.ShapeDtypeStruct((DEPTH, 6, bsz, D_MODEL), _F32),
        compiler_params=_params("arbitrary", "arbitrary"),
        name="ada_mod",
    )(c, ada_w, ada_b.reshape(DEPTH, 1, 6 * D_MODEL))


def _tile_specs(seq_len):
    main = pl.BlockSpec((TOK_TILE, SUBLANES, D_MODEL), lambda b, t: (t, b, 0))
    prev = pl.BlockSpec((1, SUBLANES, D_MODEL),
                        lambda b, t: (jnp.maximum(t * TOK_TILE - 1, 0), b, 0))
    nxt = pl.BlockSpec((1, SUBLANES, D_MODEL),
                       lambda b, t: (jnp.minimum((t + 1) * TOK_TILE, seq_len - 1), b, 0))
    return main, prev, nxt


def _mod_spec():
    return pl.BlockSpec((6, SUBLANES, D_MODEL), lambda b, t: (0, b, 0))


def _modulated_rows(x, xp_ref, xn_ref, sc, sh):
    xa = jnp.concatenate([xp_ref[...], x, xn_ref[...]], axis=0)
    u = xa * sc[None] + sh[None]
    return u.reshape(xa.shape[0] * SUBLANES, D_MODEL).astype(_BF)


def _conv3_tile(full, first, last, w, b):
    n = full.shape[0]
    zero = jnp.zeros((SUBLANES, full.shape[1]), _F32)
    return _conv3_rows(full[SUBLANES:n - SUBLANES], jnp.where(first, zero, full[:SUBLANES]),
                       jnp.where(last, zero, full[n - SUBLANES:]), w, b)


def _conv_mixer_kernel(x_ref, xp_ref, xn_ref, mod_ref, win_ref, cw_ref, cb_ref, wout_ref,
                       g_ref, b_ref, o_ref):
    t = pl.program_id(1)
    first = t == 0
    last = t == pl.num_programs(1) - 1
    sh = mod_ref[0]
    sc = 1.0 + mod_ref[1]
    gate = 1.0 + mod_ref[2]
    x = x_ref[...]
    u = _modulated_rows(x, xp_ref, xn_ref, sc, sh)
    p = jnp.dot(u, win_ref[...], preferred_element_type=_F32)
    q = p[:, D_MODEL:2 * D_MODEL] * p[:, 2 * D_MODEL:]
    cq = _conv3_tile(q, first, last, cw_ref[...], cb_ref[...])
    r = (p[SUBLANES:p.shape[0] - SUBLANES, :D_MODEL] * cq).astype(_BF)
    tok = x.shape[0] // ROW_SPLITS
    rows = tok * SUBLANES
    for i in range(ROW_SPLITS):
        xr = x[i * tok:(i + 1) * tok]
        m = jnp.dot(r[i * rows:(i + 1) * rows], wout_ref[...], preferred_element_type=_F32)
        h = DEEPNORM_ALPHA * xr + gate[None] * m.reshape(xr.shape)
        o_ref[i * tok:(i + 1) * tok] = _layer_norm(h, g_ref[...], b_ref[...])


def _conv_mixer_layer(xt, mod, w_in, conv_w, conv_b, w_out, ln_g, ln_b):
    seq_len, bsz, _ = xt.shape
    main, prev, nxt = _tile_specs(seq_len)
    return pl.pallas_call(
        _conv_mixer_kernel,
        grid=(bsz // SUBLANES, seq_len // TOK_TILE),
        in_specs=[main, prev, nxt, _mod_spec(),
                  _const_spec((D_MODEL, 3 * D_MODEL)), _const_spec((3, D_MODEL)),
                  _const_spec((1, D_MODEL)), _const_spec((D_MODEL, D_MODEL)),
                  _const_spec((1, D_MODEL)), _const_spec((1, D_MODEL))],
        out_specs=main,
        out_shape=jax.ShapeDtypeStruct(xt.shape, _F32),
        compiler_params=_params("parallel", "parallel"),
        name="conv_mixer_layer",
    )(xt, xt, xt, mod, w_in.astype(_BF), conv_w, conv_b.reshape(1, D_MODEL), w_out.astype(_BF),
      ln_g.reshape(1, D_MODEL), ln_b.reshape(1, D_MODEL))


def _ffn_kernel(x_ref, xp_ref, xn_ref, mod_ref, wup_ref, cw_ref, cb_ref, wdn_ref, g_ref, b_ref,
                o_ref):
    t = pl.program_id(1)
    first = t == 0
    last = t == pl.num_programs(1) - 1
    sh = mod_ref[3]
    sc = 1.0 + mod_ref[4]
    gate = 1.0 + mod_ref[5]
    x = x_ref[...]
    u = _modulated_rows(x, xp_ref, xn_ref, sc, sh)
    tok = x.shape[0] // ROW_SPLITS
    rows = tok * SUBLANES
    accs = [None] * ROW_SPLITS
    k0 = 0
    for kc in FFN_COL_CHUNKS:
        halves = []
        for base in (0, FFN_HIDDEN):
            lo = base + k0
            a = jnp.dot(u, wup_ref[:, lo:lo + kc], preferred_element_type=_F32)
            halves.append(_conv3_tile(a, first, last, cw_ref[:, lo:lo + kc], cb_ref[:, lo:lo + kc]))
        hid = _gelu_times(halves[0], halves[1]).astype(_BF)
        for r in range(ROW_SPLITS):
            part = jnp.dot(hid[r * rows:(r + 1) * rows], wdn_ref[k0:k0 + kc, :],
                           preferred_element_type=_F32)
            accs[r] = part if accs[r] is None else accs[r] + part
        k0 += kc
    for r in range(ROW_SPLITS):
        xr = x[r * tok:(r + 1) * tok]
        h = DEEPNORM_ALPHA * xr + gate[None] * accs[r].reshape(xr.shape)
        o_ref[r * tok:(r + 1) * tok] = _layer_norm(h, g_ref[...], b_ref[...])


def _ffn_layer(xt, mod, w_up, conv_w, conv_b, w_down, ln_g, ln_b):
    seq_len, bsz, _ = xt.shape
    main, prev, nxt = _tile_specs(seq_len)
    return pl.pallas_call(
        _ffn_kernel,
        grid=(bsz // SUBLANES, seq_len // TOK_TILE),
        in_specs=[main, prev, nxt, _mod_spec(),
                  _const_spec((D_MODEL, 2 * FFN_HIDDEN)), _const_spec((3, 2 * FFN_HIDDEN)),
                  _const_spec((1, 2 * FFN_HIDDEN)), _const_spec((FFN_HIDDEN, D_MODEL)),
                  _const_spec((1, D_MODEL)), _const_spec((1, D_MODEL))],
        out_specs=main,
        out_shape=jax.ShapeDtypeStruct(xt.shape, _F32),
        compiler_params=_params("parallel", "parallel"),
        name="conv_ffn_layer",
    )(xt, xt, xt, mod, w_up.astype(_BF), conv_w, conv_b.reshape(1, 2 * FFN_HIDDEN),
      w_down.astype(_BF), ln_g.reshape(1, D_MODEL), ln_b.reshape(1, D_MODEL))


def _s5_tables(a_re, a_im, log_dt, b_re, b_im, c_re, c_im):
    tt = CHUNK
    dt = jnp.exp(log_dt)[..., None]
    lr = a_re * dt
    li = a_im * dt
    n = jnp.arange(tt + 1, dtype=_F32)[:, None, None, None]
    mag = jnp.exp(lr[None] * n)
    pw_re = mag * jnp.cos(li[None] * n)
    pw_im = mag * jnp.sin(li[None] * n)
    lb_re, lb_im = pw_re[1], pw_im[1]
    den = a_re * a_re + a_im * a_im
    f_re = ((lb_re - 1.0) * a_re + lb_im * a_im) / den
    f_im = (lb_im * a_re - (lb_re - 1.0) * a_im) / den
    w_re = pw_re * f_re[None] - pw_im * f_im[None]
    w_im = pw_re * f_im[None] + pw_im * f_re[None]

    def kern(k):
        wr = w_re[:tt, k][:, :, None, :]
        wi = w_im[:tt, k][:, :, None, :]
        xr = c_re[k][None] * wr - c_im[k][None] * wi
        xi = c_re[k][None] * wi + c_im[k][None] * wr
        return (jnp.einsum('ngip,gpj->ngij', xr, b_re[k], precision=_HI)
                - jnp.einsum('ngip,gpj->ngij', xi, b_im[k], precision=_HI))

    kf, kb = kern(0), kern(1)
    tau = jnp.arange(tt)[:, None]
    tq = jnp.arange(tt)[None, :]
    mf = jnp.where((tq >= tau)[:, :, None, None, None], kf[jnp.clip(tq - tau, 0, tt - 1)], 0.0)
    mb = jnp.where((tau >= tq)[:, :, None, None, None], kb[jnp.clip(tau - tq, 0, tt - 1)], 0.0)
    m_t = (mf + mb).transpose(2, 1, 3, 0, 4)
    m_t = m_t.reshape(SSM_GROUPS, CHUNK_COLS, CHUNK_COLS)

    def bt(k, w_r, w_i):
        br = b_re[k].transpose(0, 2, 1)[:, None]
        bi = b_im[k].transpose(0, 2, 1)[:, None]
        wr = w_r.transpose(1, 0, 2)[:, :, None, :]
        wi = w_i.transpose(1, 0, 2)[:, :, None, :]
        re = (wr * br - wi * bi).reshape(SSM_GROUPS, CHUNK_COLS, SSM_STATE)
        im = (wr * bi + wi * br).reshape(SSM_GROUPS, CHUNK_COLS, SSM_STATE)
        return re, im

    bf_re, bf_im = bt(0, w_re[:tt, 0][::-1], w_im[:tt, 0][::-1])
    bb_re, bb_im = bt(1, w_re[:tt, 1], w_im[:tt, 1])
    bt_all = jnp.concatenate([bf_re, bb_re, bf_im, bb_im], axis=-1)

    def ct_t(k, e_r, e_i):
        cr = c_re[k][:, None]
        ci = c_im[k][:, None]
        er = e_r.transpose(1, 0, 2)[:, :, None, :]
        ei = e_i.transpose(1, 0, 2)[:, :, None, :]
        on_re = (cr * er - ci * ei).reshape(SSM_GROUPS, CHUNK_COLS, SSM_STATE)
        on_im = (-(cr * ei + ci * er)).reshape(SSM_GROUPS, CHUNK_COLS, SSM_STATE)
        return on_re, on_im

    cf_re, cf_im = ct_t(0, pw_re[1:, 0], pw_im[1:, 0])
    cb_re, cb_im = ct_t(1, pw_re[1:, 1][::-1], pw_im[1:, 1][::-1])
    ct_all = jnp.concatenate([cf_re, cb_re, cf_im, cb_im], axis=-1)

    decay = jnp.stack([jnp.concatenate([pw_re[tt, 0], pw_re[tt, 1]], axis=-1),
                       jnp.concatenate([pw_im[tt, 0], pw_im[tt, 1]], axis=-1)], axis=1)
    return m_t.astype(_BF), bt_all.astype(_BF), ct_all.astype(_BF), decay


def _s5_regroup_kernel(x_ref, mod_ref, o_ref):
    sh = mod_ref[0]
    sc = 1.0 + mod_ref[1]
    cols = x_ref.shape[0] * x_ref.shape[2]
    for tau in range(CHUNK):
        slab = (x_ref[:, tau] * sc[None] + sh[None]).reshape(cols, D_MODEL)
        o_ref[:, SSM_GROUP * tau:SSM_GROUP * (tau + 1), :] = (
            slab.T.astype(_BF).reshape(SSM_GROUPS, SSM_GROUP, cols))


def _s5_chunk_kernel(bsz, n_chunks, z_ref, mt_ref, bt_ref, ct_ref, dec_ref, y_ref,
                     loc_ref, carry_ref):
    half = SSM_STATE
    n_grp = z_ref.shape[0]
    for g in range(n_grp):
        loc_ref[g] = lax.dot_general(z_ref[g], bt_ref[g], (((0,), (0,)), ((), ())),
                                     preferred_element_type=_F32)
    a_re = [jnp.broadcast_to(dec_ref[g, 0:1, :], (bsz, 2 * half)) for g in range(n_grp)]
    a_im = [jnp.broadcast_to(dec_ref[g, 1:2, :], (bsz, 2 * half)) for g in range(n_grp)]
    is_fwd = lax.broadcasted_iota(jnp.int32, (bsz, 2 * half), 1) < half

    def step(k, state):
        rf = pl.multiple_of(k * bsz, SUBLANES)
        rb = pl.multiple_of((n_chunks - 1 - k) * bsz, SUBLANES)
        new = []
        for g in range(n_grp):
            s_re, s_im = state[2 * g], state[2 * g + 1]
            carry_ref[g, pl.ds(rf, bsz), 0:half] = s_re[:, 0:half]
            carry_ref[g, pl.ds(rb, bsz), half:2 * half] = s_re[:, half:]
            carry_ref[g, pl.ds(rf, bsz), 2 * half:3 * half] = s_im[:, 0:half]
            carry_ref[g, pl.ds(rb, bsz), 3 * half:] = s_im[:, half:]
            l_re = jnp.where(is_fwd, loc_ref[g, pl.ds(rf, bsz), 0:2 * half],
                             loc_ref[g, pl.ds(rb, bsz), 0:2 * half])
            l_im = jnp.where(is_fwd, loc_ref[g, pl.ds(rf, bsz), 2 * half:],
                             loc_ref[g, pl.ds(rb, bsz), 2 * half:])
            new.append(a_re[g] * s_re - a_im[g] * s_im + l_re)
            new.append(a_re[g] * s_im + a_im[g] * s_re + l_im)
        return tuple(new)

    zero = jnp.zeros((bsz, 2 * half), _F32)
    lax.fori_loop(0, n_chunks, step, (zero,) * (2 * n_grp), unroll=S5_SCAN_UNROLL)
    for g in range(n_grp):
        y = jnp.dot(mt_ref[g], z_ref[g], preferred_element_type=_F32)
        y = y + lax.dot_general(ct_ref[g], carry_ref[g].astype(_BF), (((1,), (1,)), ((), ())),
                                preferred_element_type=_F32)
        y_ref[g] = y.astype(_BF)


def _s5_glu_kernel(x_ref, yt_ref, mod_ref, d_ref, wglu_ref, g_ref, b_ref, o_ref):
    sh = mod_ref[0]
    sc = 1.0 + mod_ref[1]
    gate = 1.0 + mod_ref[2]
    ncb, ntok, bsz, _ = x_ref.shape
    cols = ncb * bsz
    for t0 in range(0, ntok, GLU_SUB):
        zs = []
        for tq in range(t0, t0 + GLU_SUB):
            yd = yt_ref[:, SSM_GROUP * tq:SSM_GROUP * (tq + 1), :].astype(_F32)
            yd = yd.reshape(D_MODEL, cols).T.reshape(ncb, bsz, D_MODEL)
            u = x_ref[:, tq] * sc[None] + sh[None]
            zs.append(_gelu(d_ref[...] * u + yd).reshape(cols, D_MODEL).astype(_BF))
        pz = jnp.dot(jnp.concatenate(zs, axis=0), wglu_ref[...], preferred_element_type=_F32)
        m = pz[:, :D_MODEL] * jax.nn.sigmoid(pz[:, D_MODEL:])
        for k in range(GLU_SUB):
            mt = m[cols * k:cols * (k + 1)].reshape(ncb, bsz, D_MODEL)
            h = DEEPNORM_ALPHA * x_ref[:, t0 + k] + gate[None] * mt
            o_ref[:, t0 + k] = _layer_norm(h, g_ref[...], b_ref[...])


def _s5_mixer_layer(xt, mod, tables, d_skip, w_glu, ln_g, ln_b):
    seq_len, bsz, _ = xt.shape
    n_chunks = seq_len // CHUNK
    ncb = LANES // bsz
    cols = n_chunks * bsz
    x4 = xt.reshape(n_chunks, CHUNK, bsz, D_MODEL)
    mod_full = pl.BlockSpec((6, bsz, D_MODEL), lambda *_: (0, 0, 0))
    zt = pl.pallas_call(
        _s5_regroup_kernel,
        grid=(n_chunks // ncb,),
        in_specs=[pl.BlockSpec((ncb, CHUNK, bsz, D_MODEL), lambda i: (i, 0, 0, 0)), mod_full],
        out_specs=pl.BlockSpec((SSM_GROUPS, CHUNK_COLS, LANES), lambda i: (0, 0, i)),
        out_shape=jax.ShapeDtypeStruct((SSM_GROUPS, CHUNK_COLS, cols), _BF),
        compiler_params=_params("parallel"),
        name="s5_regroup",
    )(x4, mod)
    gs = S5_GROUPS_PER_STEP
    grp = lambda r, c: pl.BlockSpec((gs, r, c), lambda g: (g, 0, 0))
    m_t, bt, ct_t, decay = tables
    yt = pl.pallas_call(
        functools.partial(_s5_chunk_kernel, bsz, n_chunks),
        grid=(SSM_GROUPS // gs,),
        in_specs=[grp(CHUNK_COLS, cols), grp(CHUNK_COLS, CHUNK_COLS), grp(CHUNK_COLS, CHUNK_COLS),
                  grp(CHUNK_COLS, CHUNK_COLS), grp(2, 2 * SSM_STATE)],
        out_specs=grp(CHUNK_COLS, cols),
        out_shape=jax.ShapeDtypeStruct((SSM_GROUPS, CHUNK_COLS, cols), _BF),
        scratch_shapes=[pltpu.VMEM((gs, cols, CHUNK_COLS), _F32),
                        pltpu.VMEM((gs, cols, CHUNK_COLS), _F32)],
        compiler_params=_params("parallel"),
        name="s5_chunk_scan",
    )(zt, m_t, bt, ct_t, decay)
    xblk = pl.BlockSpec((ncb, GLU_TOKS, bsz, D_MODEL), lambda i, q: (i, q, 0, 0))
    out = pl.pallas_call(
        _s5_glu_kernel,
        grid=(n_chunks // ncb, CHUNK // GLU_TOKS),
        in_specs=[xblk,
                  pl.BlockSpec((SSM_GROUPS, SSM_GROUP * GLU_TOKS, LANES), lambda i, q: (0, q, i)),
                  mod_full, _const_spec((1, D_MODEL)), _const_spec((D_MODEL, 2 * D_MODEL)),
                  _const_spec((1, D_MODEL)), _const_spec((1, D_MODEL))],
        out_specs=xblk,
        out_shape=jax.ShapeDtypeStruct(x4.shape, _F32),
        compiler_params=_params("parallel", "parallel"),
        name="s5_glu_layer",
    )(x4, yt, mod, d_skip.reshape(1, D_MODEL), w_glu.astype(_BF),
      ln_g.reshape(1, D_MODEL), ln_b.reshape(1, D_MODEL))
    return out.reshape(xt.shape)


def _trunk(x, mod, p, s5_tables):
    xt = x.transpose(1, 0, 2)
    for i in range(DEPTH):
        j = i // 2
        if i % 2 == 0:
            xt = _conv_mixer_layer(xt, mod[i], p["sc_w_in"][j], p["sc_conv_w"][j],
                                   p["sc_conv_b"][j], p["sc_w_out"][j], p["ln1_g"][i], p["ln1_b"][i])
        else:
            xt = _s5_mixer_layer(xt, mod[i], s5_tables[j], p["s5_d"][j], p["s5_w_glu"][j],
                                 p["ln1_g"][i], p["ln1_b"][i])
        xt = _ffn_layer(xt, mod[i], p["ffn_w_up"][i], p["ffn_conv_w"][i], p["ffn_conv_b"][i],
                        p["ffn_w_down"][i], p["ln2_g"][i], p["ln2_b"][i])
    return xt.transpose(1, 0, 2)


def kernel(x_prompt, x_sample, c_prompt, c_sample, ada_w, ada_b, ln1_g, ln1_b, ln2_g, ln2_b, sc_w_in, sc_conv_w, sc_conv_b, sc_w_out, s5_a_re, s5_a_im, s5_log_dt, s5_b_re, s5_b_im, s5_c_re, s5_c_im, s5_d, s5_w_glu, ffn_w_up, ffn_conv_w, ffn_conv_b, ffn_w_down):
    p = dict(ada_w=ada_w, ada_b=ada_b, ln1_g=ln1_g, ln1_b=ln1_b, ln2_g=ln2_g, ln2_b=ln2_b,
             sc_w_in=sc_w_in, sc_conv_w=sc_conv_w, sc_conv_b=sc_conv_b, sc_w_out=sc_w_out,
             s5_d=s5_d, s5_w_glu=s5_w_glu, ffn_w_up=ffn_w_up, ffn_conv_w=ffn_conv_w,
             ffn_conv_b=ffn_conv_b, ffn_w_down=ffn_w_down)
    s5_tables = [_s5_tables(s5_a_re[j], s5_a_im[j], s5_log_dt[j], s5_b_re[j], s5_b_im[j],
                            s5_c_re[j], s5_c_im[j]) for j in range(s5_a_re.shape[0])]
    n_prompt = c_prompt.shape[0]
    mod = _ada(jnp.concatenate([c_prompt, c_sample], axis=0), ada_w, ada_b)
    y_prompt = _trunk(x_prompt, mod[:, :, :n_prompt], p, s5_tables)
    y_sample = _trunk(x_sample, mod[:, :, n_prompt:], p, s5_tables)
    return (y_prompt, y_sample)
```

```python
import functools
import math

import jax
import jax.numpy as jnp
from jax import lax
from jax.experimental import pallas as pl
from jax.experimental.pallas import tpu as pltpu

D_MODEL = 1024
DEPTH = 4
FFN_HIDDEN = 2816
SSM_GROUP = 16
SSM_GROUPS = D_MODEL // SSM_GROUP
SSM_STATE = 64
DEEPNORM_ALPHA = (2 * DEPTH) ** 0.25
LN_EPS = 1e-5

SUBLANES = 8
LANES = 128
CHUNK = 16
CHUNK_COLS = CHUNK * SSM_GROUP
TOK_TILE = 128
GLU_TOKS = 16
GLU_SUB = 2
S5_GROUPS_PER_STEP = 2
S5_SCAN_UNROLL = 2
FFN_COL_CHUNKS = (1280, 1536)
ROW_SPLITS = 4
VMEM_LIMIT = 56 * 1024 * 1024

_HI = lax.Precision.HIGHEST
_BF = jnp.bfloat16
_F32 = jnp.float32


def _const_spec(shape):
    nd = len(shape)
    return pl.BlockSpec(shape, lambda *_: (0,) * nd, pipeline_mode=pl.Buffered(1))


def _params(*sem):
    return pltpu.CompilerParams(dimension_semantics=sem, vmem_limit_bytes=VMEM_LIMIT)


_GELU_C1 = math.sqrt(2.0 / math.pi)
_GELU_C2 = _GELU_C1 * 0.044715


def _gelu_half_gate(x):
    return 0.5 + 0.5 * jnp.tanh(x * (_GELU_C1 + _GELU_C2 * (x * x)))


def _gelu(x):
    return x * _gelu_half_gate(x)


def _gelu_times(x, v):
    return (x * v) * _gelu_half_gate(x)


def _layer_norm(h, g, b):
    mu = jnp.mean(h, axis=-1, keepdims=True)
    d = h - mu
    var = jnp.mean(d * d, axis=-1, keepdims=True)
    return d * lax.rsqrt(var + LN_EPS) * g + b


def _conv3_rows(cur, before, after, w, b):
    up = jnp.concatenate([before, cur[:-SUBLANES]], axis=0)
    dn = jnp.concatenate([cur[SUBLANES:], after], axis=0)
    return up * w[0:1, :] + cur * w[1:2, :] + dn * w[2:3, :] + b


def _ada_kernel(c_ref, w_ref, b_ref, o_ref):
    c = c_ref[...]
    ca = c * jax.nn.sigmoid(c)
    o_ref[...] = jnp.dot(ca, w_ref[...], precision=_HI, preferred_element_type=_F32) + b_ref[...]


def _ada(c, ada_w, ada_b):
    bsz = c.shape[0]
    return pl.pallas_call(
        _ada_kernel,
        grid=(DEPTH, 6),
        in_specs=[
            pl.BlockSpec((bsz, D_MODEL), lambda i, n: (0, 0)),
            pl.BlockSpec((None, D_MODEL, D_MODEL), lambda i, n: (i, 0, n)),
            pl.BlockSpec((None, 1, D_MODEL), lambda i, n: (i, 0, n)),
        ],
        out_specs=pl.BlockSpec((None, None, bsz, D_MODEL), lambda i, n: (i, n, 0, 0)),
        out_shape=jax.ShapeDtypeStruct((DEPTH, 6, bsz, D_MODEL), _F32),
        compiler_params=_params("arbitrary", "arbitrary"),
        name="ada_mod",
    )(c, ada_w, ada_b.reshape(DEPTH, 1, 6 * D_MODEL))


def _tile_specs(seq_len):
    main = pl.BlockSpec((TOK_TILE, SUBLANES, D_MODEL), lambda b, t: (t, b, 0))
    prev = pl.BlockSpec((1, SUBLANES, D_MODEL),
                        lambda b, t: (jnp.maximum(t * TOK_TILE - 1, 0), b, 0))
    nxt = pl.BlockSpec((1, SUBLANES, D_MODEL),
                       lambda b, t: (jnp.minimum((t + 1) * TOK_TILE, seq_len - 1), b, 0))
    return main, prev, nxt


def _mod_spec():
    return pl.BlockSpec((6, SUBLANES, D_MODEL), lambda b, t: (0, b, 0))


def _modulated_rows(x, xp_ref, xn_ref, sc, sh):
    xa = jnp.concatenate([xp_ref[...], x, xn_ref[...]], axis=0)
    u = xa * sc[None] + sh[None]
    return u.reshape(xa.shape[0] * SUBLANES, D_MODEL).astype(_BF)


def _conv3_tile(full, first, last, w, b):
    n = full.shape[0]
    zero = jnp.zeros((SUBLANES, full.shape[1]), _F32)
    return _conv3_rows(full[SUBLANES:n - SUBLANES], jnp.where(first, zero, full[:SUBLANES]),
                       jnp.where(last, zero, full[n - SUBLANES:]), w, b)


def _conv_mixer_kernel(x_ref, xp_ref, xn_ref, mod_ref, win_ref, cw_ref, cb_ref, wout_ref,
                       g_ref, b_ref, o_ref):
    t = pl.program_id(1)
    first = t == 0
    last = t == pl.num_programs(1) - 1
    sh = mod_ref[0]
    sc = 1.0 + mod_ref[1]
    gate = 1.0 + mod_ref[2]
    x = x_ref[...]
    u = _modulated_rows(x, xp_ref, xn_ref, sc, sh)
    p = jnp.dot(u, win_ref[...], preferred_element_type=_F32)
    q = p[:, D_MODEL:2 * D_MODEL] * p[:, 2 * D_MODEL:]
    cq = _conv3_tile(q, first, last, cw_ref[...], cb_ref[...])
    r = (p[SUBLANES:p.shape[0] - SUBLANES, :D_MODEL] * cq).astype(_BF)
    tok = x.shape[0] // ROW_SPLITS
    rows = tok * SUBLANES
    for i in range(ROW_SPLITS):
        xr = x[i * tok:(i + 1) * tok]
        m = jnp.dot(r[i * rows:(i + 1) * rows], wout_ref[...], preferred_element_type=_F32)
        h = DEEPNORM_ALPHA * xr + gate[None] * m.reshape(xr.shape)
        o_ref[i * tok:(i + 1) * tok] = _layer_norm(h, g_ref[...], b_ref[...])


def _conv_mixer_layer(xt, mod, w_in, conv_w, conv_b, w_out, ln_g, ln_b):
    seq_len, bsz, _ = xt.shape
    main, prev, nxt = _tile_specs(seq_len)
    return pl.pallas_call(
        _conv_mixer_kernel,
        grid=(bsz // SUBLANES, seq_len // TOK_TILE),
        in_specs=[main, prev, nxt, _mod_spec(),
                  _const_spec((D_MODEL, 3 * D_MODEL)), _const_spec((3, D_MODEL)),
                  _const_spec((1, D_MODEL)), _const_spec((D_MODEL, D_MODEL)),
                  _const_spec((1, D_MODEL)), _const_spec((1, D_MODEL))],
        out_specs=main,
        out_shape=jax.ShapeDtypeStruct(xt.shape, _F32),
        compiler_params=_params("parallel", "parallel"),
        name="conv_mixer_layer",
    )(xt, xt, xt, mod, w_in.astype(_BF), conv_w, conv_b.reshape(1, D_MODEL), w_out.astype(_BF),
      ln_g.reshape(1, D_MODEL), ln_b.reshape(1, D_MODEL))


def _ffn_kernel(x_ref, xp_ref, xn_ref, mod_ref, wup_ref, cw_ref, cb_ref, wdn_ref, g_ref, b_ref,
                o_ref):
    t = pl.program_id(1)
    first = t == 0
    last = t == pl.num_programs(1) - 1
    sh = mod_ref[3]
    sc = 1.0 + mod_ref[4]
    gate = 1.0 + mod_ref[5]
    x = x_ref[...]
    u = _modulated_rows(x, xp_ref, xn_ref, sc, sh)
    tok = x.shape[0] // ROW_SPLITS
    rows = tok * SUBLANES
    accs = [None] * ROW_SPLITS
    k0 = 0
    for kc in FFN_COL_CHUNKS:
        halves = []
        for base in (0, FFN_HIDDEN):
            lo = base + k0
            a = jnp.dot(u, wup_ref[:, lo:lo + kc], preferred_element_type=_F32)
            halves.append(_conv3_tile(a, first, last, cw_ref[:, lo:lo + kc], cb_ref[:, lo:lo + kc]))
        hid = _gelu_times(halves[0], halves[1]).astype(_BF)
        for r in range(ROW_SPLITS):
            part = jnp.dot(hid[r * rows:(r + 1) * rows], wdn_ref[k0:k0 + kc, :],
                           preferred_element_type=_F32)
            accs[r] = part if accs[r] is None else accs[r] + part
        k0 += kc
    for r in range(ROW_SPLITS):
        xr = x[r * tok:(r + 1) * tok]
        h = DEEPNORM_ALPHA * xr + gate[None] * accs[r].reshape(xr.shape)
        o_ref[r * tok:(r + 1) * tok] = _layer_norm(h, g_ref[...], b_ref[...])


def _ffn_layer(xt, mod, w_up, conv_w, conv_b, w_down, ln_g, ln_b):
    seq_len, bsz, _ = xt.shape
    main, prev, nxt = _tile_specs(seq_len)
    return pl.pallas_call(
        _ffn_kernel,
        grid=(bsz // SUBLANES, seq_len // TOK_TILE),
        in_specs=[main, prev, nxt, _mod_spec(),
                  _const_spec((D_MODEL, 2 * FFN_HIDDEN)), _const_spec((3, 2 * FFN_HIDDEN)),
                  _const_spec((1, 2 * FFN_HIDDEN)), _const_spec((FFN_HIDDEN, D_MODEL)),
                  _const_spec((1, D_MODEL)), _const_spec((1, D_MODEL))],
        out_specs=main,
        out_shape=jax.ShapeDtypeStruct(xt.shape, _F32),
        compiler_params=_params("parallel", "parallel"),
        name="conv_ffn_layer",
    )(xt, xt, xt, mod, w_up.astype(_BF), conv_w, conv_b.reshape(1, 2 * FFN_HIDDEN),
      w_down.astype(_BF), ln_g.reshape(1, D_MODEL), ln_b.reshape(1, D_MODEL))


def _s5_tables(a_re, a_im, log_dt, b_re, b_im, c_re, c_im):
    n_lay, n_dir, n_grp, n_st = a_re.shape
    both = n_dir * n_st
    rows = n_lay * n_grp

    def states_last(v):
        return v.transpose(0, 2, 3, 1, 4).reshape(rows, v.shape[3], both)

    ldt = jnp.broadcast_to(log_dt[..., None], a_re.shape)
    par = states_last(jnp.stack([a_re, a_im, ldt], axis=3))
    par_col = par.transpose(0, 2, 1)
    b_col = jnp.stack([b_re, b_im], axis=2)
    b_col = b_col.transpose(0, 3, 2, 1, 4, 5).reshape(rows, 2, both, SSM_GROUP)
    b_row = jnp.stack([states_last(b_re.transpose(0, 1, 2, 4, 3)),
                       states_last(b_im.transpose(0, 1, 2, 4, 3))], axis=1)
    c_row = jnp.stack([states_last(c_re), states_last(c_im)], axis=1)
    blk = lambda *shape: pl.BlockSpec((None,) + shape, lambda g: (g,) + (0,) * len(shape))
    sq = (CHUNK_COLS, CHUNK_COLS)
    m_t, bt, ct_t, decay = pl.pallas_call(
        _s5_table_kernel,
        grid=(rows,),
        in_specs=[blk(3, both), blk(both, 3), blk(2, both, SSM_GROUP), blk(2, SSM_GROUP, both),
                  blk(2, SSM_GROUP, both)],
        out_specs=[blk(*sq), blk(*sq), blk(*sq), blk(2, both)],
        out_shape=[jax.ShapeDtypeStruct((rows,) + sq, _BF)] * 3
        + [jax.ShapeDtypeStruct((rows, 2, both), _F32)],
        compiler_params=_params("parallel"),
        name="s5_tables",
    )(par, par_col, b_col, b_row, c_row)
    per_layer = lambda v: v.reshape((n_lay, n_grp) + v.shape[1:])
    return [tuple(per_layer(v)[j] for v in (m_t, bt, ct_t, decay)) for j in range(n_lay)]


def _s5_table_kernel(par_ref, parc_ref, bcol_ref, brow_ref, crow_ref, mt_ref, bt_ref, ct_ref,
                     dec_ref):
    tt = CHUNK
    half = SSM_STATE
    both = 2 * half

    def discretise(a_re, a_im, log_dt):
        dt = jnp.exp(log_dt)
        lr, li = a_re * dt, a_im * dt

        def power(n):
            mag = jnp.exp(lr * n)
            return mag * jnp.cos(li * n), mag * jnp.sin(li * n)

        lb_re, lb_im = power(1.0)
        den = a_re * a_re + a_im * a_im
        f_re = ((lb_re - 1.0) * a_re + lb_im * a_im) / den
        f_im = (lb_im * a_re - (lb_re - 1.0) * a_im) / den
        return power, f_re, f_im

    power, f_re, f_im = discretise(par_ref[0:1, :], par_ref[1:2, :], par_ref[2:3, :])
    tok = lax.broadcasted_iota(jnp.int32, (tt, both), 0).astype(_F32)
    fwd = lax.broadcasted_iota(jnp.int32, (tt, both), 1) < half
    p_re, p_im = power(jnp.where(fwd, tt - 1.0 - tok, tok))
    w_re = (p_re * f_re - p_im * f_im)[:, None, :]
    w_im = (p_re * f_im + p_im * f_re)[:, None, :]
    b_re, b_im = brow_ref[0][None], brow_ref[1][None]
    bt_ref[:, 0:both] = (w_re * b_re - w_im * b_im).reshape(CHUNK_COLS, both).astype(_BF)
    bt_ref[:, both:] = (w_re * b_im + w_im * b_re).reshape(CHUNK_COLS, both).astype(_BF)
    e_re, e_im = power(jnp.where(fwd, tok + 1.0, tt - tok))
    e_re, e_im = e_re[:, None, :], e_im[:, None, :]
    c_re, c_im = crow_ref[0], crow_ref[1]
    ct_ref[:, 0:both] = (c_re[None] * e_re - c_im[None] * e_im).reshape(CHUNK_COLS, both).astype(_BF)
    ct_ref[:, both:] = (-(c_re[None] * e_im + c_im[None] * e_re)).reshape(CHUNK_COLS, both).astype(_BF)
    d_re, d_im = power(float(tt))
    dec_ref[0:1, :] = d_re
    dec_ref[1:2, :] = d_im

    n_s = 2 * tt
    power, f_re, f_im = discretise(parc_ref[:, 0:1], parc_ref[:, 1:2], parc_ref[:, 2:3])
    s_idx = lax.broadcasted_iota(jnp.int32, (both, n_s), 1).astype(_F32)
    fwd = lax.broadcasted_iota(jnp.int32, (both, n_s), 0) < half
    expo = jnp.where(fwd, tt - 1.0 - s_idx, s_idx - (tt - 1.0))
    live = expo >= 0.0
    p_re, p_im = power(jnp.maximum(expo, 0.0))
    w_re = jnp.where(live, p_re * f_re - p_im * f_im, 0.0)
    w_im = jnp.where(live, p_re * f_im + p_im * f_re, 0.0)
    width = n_s * SSM_GROUP
    lane = lax.broadcasted_iota(jnp.int32, (n_s, width), 1)
    spread_s = (lane // SSM_GROUP == lax.broadcasted_iota(jnp.int32, (n_s, width), 0)).astype(_F32)
    lane = lax.broadcasted_iota(jnp.int32, (SSM_GROUP, width), 1)
    spread_j = (lane % SSM_GROUP == lax.broadcasted_iota(jnp.int32, (SSM_GROUP, width), 0)).astype(_F32)
    dot = functools.partial(jnp.dot, precision=_HI, preferred_element_type=_F32)
    w_re, w_im = dot(w_re, spread_s), dot(w_im, spread_s)
    b_re, b_im = dot(bcol_ref[0], spread_j), dot(bcol_ref[1], spread_j)
    strip = dot(c_re, w_re * b_re - w_im * b_im) - dot(c_im, w_re * b_im + w_im * b_re)
    for t in range(tt):
        lo = SSM_GROUP * (tt - 1 - t)
        mt_ref[SSM_GROUP * t:SSM_GROUP * (t + 1), :] = strip[:, lo:lo + CHUNK_COLS].astype(_BF)


def _s5_regroup_kernel(x_ref, mod_ref, o_ref):
    sh = mod_ref[0]
    sc = 1.0 + mod_ref[1]
    cols = x_ref.shape[0] * x_ref.shape[2]
    for tau in range(CHUNK):
        slab = (x_ref[:, tau] * sc[None] + sh[None]).reshape(cols, D_MODEL)
        o_ref[:, SSM_GROUP * tau:SSM_GROUP * (tau + 1), :] = (
            slab.T.astype(_BF).reshape(SSM_GROUPS, SSM_GROUP, cols))


def _s5_chunk_kernel(bsz, n_chunks, z_ref, mt_ref, bt_ref, ct_ref, dec_ref, y_ref,
                     loc_ref, carry_ref):
    half = SSM_STATE
    n_grp = z_ref.shape[0]
    for g in range(n_grp):
        loc_ref[g] = lax.dot_general(z_ref[g], bt_ref[g], (((0,), (0,)), ((), ())),
                                     preferred_element_type=_F32)
    a_re = [jnp.broadcast_to(dec_ref[g, 0:1, :], (bsz, 2 * half)) for g in range(n_grp)]
    a_im = [jnp.broadcast_to(dec_ref[g, 1:2, :], (bsz, 2 * half)) for g in range(n_grp)]
    is_fwd = lax.broadcasted_iota(jnp.int32, (bsz, 2 * half), 1) < half

    def step(k, state):
        rf = pl.multiple_of(k * bsz, SUBLANES)
        rb = pl.multiple_of((n_chunks - 1 - k) * bsz, SUBLANES)
        new = []
        for g in range(n_grp):
            s_re, s_im = state[2 * g], state[2 * g + 1]
            carry_ref[g, pl.ds(rf, bsz), 0:half] = s_re[:, 0:half]
            carry_ref[g, pl.ds(rb, bsz), half:2 * half] = s_re[:, half:]
            carry_ref[g, pl.ds(rf, bsz), 2 * half:3 * half] = s_im[:, 0:half]
            carry_ref[g, pl.ds(rb, bsz), 3 * half:] = s_im[:, half:]
            l_re = jnp.where(is_fwd, loc_ref[g, pl.ds(rf, bsz), 0:2 * half],
                             loc_ref[g, pl.ds(rb, bsz), 0:2 * half])
            l_im = jnp.where(is_fwd, loc_ref[g, pl.ds(rf, bsz), 2 * half:],
                             loc_ref[g, pl.ds(rb, bsz), 2 * half:])
            new.append(a_re[g] * s_re - a_im[g] * s_im + l_re)
            new.append(a_re[g] * s_im + a_im[g] * s_re + l_im)
        return tuple(new)

    zero = jnp.zeros((bsz, 2 * half), _F32)
    lax.fori_loop(0, n_chunks, step, (zero,) * (2 * n_grp), unroll=S5_SCAN_UNROLL)
    for g in range(n_grp):
        y = jnp.dot(mt_ref[g], z_ref[g], preferred_element_type=_F32)
        y = y + lax.dot_general(ct_ref[g], carry_ref[g].astype(_BF), (((1,), (1,)), ((), ())),
                                preferred_element_type=_F32)
        y_ref[g] = y.astype(_BF)


def _s5_glu_kernel(x_ref, yt_ref, mod_ref, d_ref, wglu_ref, g_ref, b_ref, o_ref):
    sh = mod_ref[0]
    sc = 1.0 + mod_ref[1]
    gate = 1.0 + mod_ref[2]
    ncb, ntok, bsz, _ = x_ref.shape
    cols = ncb * bsz
    for t0 in range(0, ntok, GLU_SUB):
        zs = []
        for tq in range(t0, t0 + GLU_SUB):
            yd = yt_ref[:, SSM_GROUP * tq:SSM_GROUP * (tq + 1), :].astype(_F32)
            yd = yd.reshape(D_MODEL, cols).T.reshape(ncb, bsz, D_MODEL)
            u = x_ref[:, tq] * sc[None] + sh[None]
            zs.append(_gelu(d_ref[...] * u + yd).reshape(cols, D_MODEL).astype(_BF))
        pz = jnp.dot(jnp.concatenate(zs, axis=0), wglu_ref[...], preferred_element_type=_F32)
        m = pz[:, :D_MODEL] * jax.nn.sigmoid(pz[:, D_MODEL:])
        for k in range(GLU_SUB):
            mt = m[cols * k:cols * (k + 1)].reshape(ncb, bsz, D_MODEL)
            h = DEEPNORM_ALPHA * x_ref[:, t0 + k] + gate[None] * mt
            o_ref[:, t0 + k] = _layer_norm(h, g_ref[...], b_ref[...])


def _s5_mixer_layer(xt, mod, tables, d_skip, w_glu, ln_g, ln_b):
    seq_len, bsz, _ = xt.shape
    n_chunks = seq_len // CHUNK
    ncb = LANES // bsz
    cols = n_chunks * bsz
    x4 = xt.reshape(n_chunks, CHUNK, bsz, D_MODEL)
    mod_full = pl.BlockSpec((6, bsz, D_MODEL), lambda *_: (0, 0, 0))
    zt = pl.pallas_call(
        _s5_regroup_kernel,
        grid=(n_chunks // ncb,),
        in_specs=[pl.BlockSpec((ncb, CHUNK, bsz, D_MODEL), lambda i: (i, 0, 0, 0)), mod_full],
        out_specs=pl.BlockSpec((SSM_GROUPS, CHUNK_COLS, LANES), lambda i: (0, 0, i)),
        out_shape=jax.ShapeDtypeStruct((SSM_GROUPS, CHUNK_COLS, cols), _BF),
        compiler_params=_params("parallel"),
        name="s5_regroup",
    )(x4, mod)
    gs = S5_GROUPS_PER_STEP
    grp = lambda r, c: pl.BlockSpec((gs, r, c), lambda g: (g, 0, 0))
    m_t, bt, ct_t, decay = tables
    yt = pl.pallas_call(
        functools.partial(_s5_chunk_kernel, bsz, n_chunks),
        grid=(SSM_GROUPS // gs,),
        in_specs=[grp(CHUNK_COLS, cols), grp(CHUNK_COLS, CHUNK_COLS), grp(CHUNK_COLS, CHUNK_COLS),
                  grp(CHUNK_COLS, CHUNK_COLS), grp(2, 2 * SSM_STATE)],
        out_specs=grp(CHUNK_COLS, cols),
        out_shape=jax.ShapeDtypeStruct((SSM_GROUPS, CHUNK_COLS, cols), _BF),
        scratch_shapes=[pltpu.VMEM((gs, cols, CHUNK_COLS), _F32),
                        pltpu.VMEM((gs, cols, CHUNK_COLS), _F32)],
        compiler_params=_params("parallel"),
        name="s5_chunk_scan",
    )(zt, m_t, bt, ct_t, decay)
    xblk = pl.BlockSpec((ncb, GLU_TOKS, bsz, D_MODEL), lambda i, q: (i, q, 0, 0))
    out = pl.pallas_call(
        _s5_glu_kernel,
        grid=(n_chunks // ncb, CHUNK // GLU_TOKS),
        in_specs=[xblk,
                  pl.BlockSpec((SSM_GROUPS, SSM_GROUP * GLU_TOKS, LANES), lambda i, q: (0, q, i)),
                  mod_full, _const_spec((1, D_MODEL)), _const_spec((D_MODEL, 2 * D_MODEL)),
                  _const_spec((1, D_MODEL)), _const_spec((1, D_MODEL))],
        out_specs=xblk,
        out_shape=jax.ShapeDtypeStruct(x4.shape, _F32),
        compiler_params=_params("parallel", "parallel"),
        name="s5_glu_layer",
    )(x4, yt, mod, d_skip.reshape(1, D_MODEL), w_glu.astype(_BF),
      ln_g.reshape(1, D_MODEL), ln_b.reshape(1, D_MODEL))
    return out.reshape(xt.shape)


def _trunk(x, mod, p, s5_tables):
    xt = x.transpose(1, 0, 2)
    for i in range(DEPTH):
        j = i // 2
        if i % 2 == 0:
            xt = _conv_mixer_layer(xt, mod[i], p["sc_w_in"][j], p["sc_conv_w"][j],
                                   p["sc_conv_b"][j], p["sc_w_out"][j], p["ln1_g"][i], p["ln1_b"][i])
        else:
            xt = _s5_mixer_layer(xt, mod[i], s5_tables[j], p["s5_d"][j], p["s5_w_glu"][j],
                                 p["ln1_g"][i], p["ln1_b"][i])
        xt = _ffn_layer(xt, mod[i], p["ffn_w_up"][i], p["ffn_conv_w"][i], p["ffn_conv_b"][i],
                        p["ffn_w_down"][i], p["ln2_g"][i], p["ln2_b"][i])
    return xt.transpose(1, 0, 2)


def kernel(x_prompt, x_sample, c_prompt, c_sample, ada_w, ada_b, ln1_g, ln1_b, ln2_g, ln2_b, sc_w_in, sc_conv_w, sc_conv_b, sc_w_out, s5_a_re, s5_a_im, s5_log_dt, s5_b_re, s5_b_im, s5_c_re, s5_c_im, s5_d, s5_w_glu, ffn_w_up, ffn_conv_w, ffn_conv_b, ffn_w_down):
    p = dict(ada_w=ada_w, ada_b=ada_b, ln1_g=ln1_g, ln1_b=ln1_b, ln2_g=ln2_g, ln2_b=ln2_b,
             sc_w_in=sc_w_in, sc_conv_w=sc_conv_w, sc_conv_b=sc_conv_b, sc_w_out=sc_w_out,
             s5_d=s5_d, s5_w_glu=s5_w_glu, ffn_w_up=ffn_w_up, ffn_conv_w=ffn_conv_w,
             ffn_conv_b=ffn_conv_b, ffn_w_down=ffn_w_down)
    s5_tables = _s5_tables(s5_a_re, s5_a_im, s5_log_dt, s5_b_re, s5_b_im, s5_c_re, s5_c_im)
    n_prompt = c_prompt.shape[0]
    mod = _ada(jnp.concatenate([c_prompt, c_sample], axis=0), ada_w, ada_b)
    y_prompt = _trunk(x_prompt, mod[:, :, :n_prompt], p, s5_tables)
    y_sample = _trunk(x_sample, mod[:, :, n_prompt:], p, s5_tables)
    return (y_prompt, y_sample)
```

```python
import functools
import math

import jax
import jax.numpy as jnp
from jax import lax
from jax.experimental import pallas as pl
from jax.experimental.pallas import tpu as pltpu

D_MODEL = 1024
DEPTH = 4
FFN_HIDDEN = 2816
SSM_GROUP = 16
SSM_GROUPS = D_MODEL // SSM_GROUP
SSM_STATE = 64
DEEPNORM_ALPHA = (2 * DEPTH) ** 0.25
LN_EPS = 1e-5

SUBLANES = 8
LANES = 128
CHUNK = 16
CHUNK_COLS = CHUNK * SSM_GROUP
TOK_TILE = 128
GLU_TOKS = 16
GLU_SUB = 2
S5_GROUPS_PER_STEP = 2
S5_SCAN_UNROLL = 2
FFN_COL_CHUNKS = (1280, 1536)
ROW_SPLITS = 4
VMEM_LIMIT = 56 * 1024 * 1024

_HI = lax.Precision.HIGHEST
_BF = jnp.bfloat16
_F32 = jnp.float32


def _const_spec(shape):
    nd = len(shape)
    return pl.BlockSpec(shape, lambda *_: (0,) * nd, pipeline_mode=pl.Buffered(1))


def _params(*sem):
    return pltpu.CompilerParams(dimension_semantics=sem, vmem_limit_bytes=VMEM_LIMIT)


_GELU_C1 = math.sqrt(2.0 / math.pi)
_GELU_C2 = _GELU_C1 * 0.044715


def _gelu_half_gate(x):
    return 0.5 + 0.5 * jnp.tanh(x * (_GELU_C1 + _GELU_C2 * (x * x)))


def _gelu(x):
    return x * _gelu_half_gate(x)


def _gelu_times(x, v):
    return (x * v) * _gelu_half_gate(x)


def _layer_norm(h, g, b):
    mu = jnp.mean(h, axis=-1, keepdims=True)
    d = h - mu
    var = jnp.mean(d * d, axis=-1, keepdims=True)
    return d * lax.rsqrt(var + LN_EPS) * g + b


def _conv3_rows(cur, before, after, w, b):
    up = jnp.concatenate([before, cur[:-SUBLANES]], axis=0)
    dn = jnp.concatenate([cur[SUBLANES:], after], axis=0)
    return up * w[0:1, :] + cur * w[1:2, :] + dn * w[2:3, :] + b


def _ada_kernel(c_ref, w_ref, b_ref, o_ref):
    c = c_ref[...]
    ca = c * jax.nn.sigmoid(c)
    o_ref[...] = jnp.dot(ca, w_ref[...], precision=_HI, preferred_element_type=_F32) + b_ref[...]


def _ada(c, ada_w, ada_b):
    bsz = c.shape[0]
    return pl.pallas_call(
        _ada_kernel,
        grid=(DEPTH, 6),
        in_specs=[
            pl.BlockSpec((bsz, D_MODEL), lambda i, n: (0, 0)),
            pl.BlockSpec((None, D_MODEL, D_MODEL), lambda i, n: (i, 0, n)),
            pl.BlockSpec((None, 1, D_MODEL), lambda i, n: (i, 0, n)),
        ],
        out_specs=pl.BlockSpec((None, None, bsz, D_MODEL), lambda i, n: (i, n, 0, 0)),
        out_shape=jax.ShapeDtypeStruct((DEPTH, 6, bsz, D_MODEL), _F32),
        compiler_params=_params("arbitrary", "arbitrary"),
        name="ada_mod",
    )(c, ada_w, ada_b.reshape(DEPTH, 1, 6 * D_MODEL))


def _tile_specs(seq_len):
    main = pl.BlockSpec((TOK_TILE, SUBLANES, D_MODEL), lambda b, t: (t, b, 0))
    prev = pl.BlockSpec((1, SUBLANES, D_MODEL),
                        lambda b, t: (jnp.maximum(t * TOK_TILE - 1, 0), b, 0))
    nxt = pl.BlockSpec((1, SUBLANES, D_MODEL),
                       lambda b, t: (jnp.minimum((t + 1) * TOK_TILE, seq_len - 1), b, 0))
    return main, prev, nxt


def _mod_spec():
    return pl.BlockSpec((6, SUBLANES, D_MODEL), lambda b, t: (0, b, 0))


def _modulated_rows(xa, sc, sh):
    u = xa * sc[None] + sh[None]
    return u.reshape(xa.shape[0] * SUBLANES, D_MODEL).astype(_BF)


def _linear_step():
    return pl.program_id(0) * pl.num_programs(1) + pl.program_id(1)


def _gather_copies(x_hbm, buf, sem, slot, step, seq_len):
    tiles = seq_len // TOK_TILE
    seq0 = (step // tiles) * SUBLANES
    t0 = (step % tiles) * TOK_TILE
    spans = ((jnp.maximum(t0 - 1, 0), 1, 0), (t0, TOK_TILE, 1),
             (jnp.minimum(t0 + TOK_TILE, seq_len - 1), 1, TOK_TILE + 1))
    return [pltpu.make_async_copy(x_hbm.at[seq0 + b, pl.ds(src, n), :],
                                  buf.at[slot, pl.ds(dst, n), b, :], sem.at[slot])
            for b in range(SUBLANES) for src, n, dst in spans]


def _scatter_copies(buf, o_hbm, sem, slot, step, seq_len):
    tiles = seq_len // TOK_TILE
    seq0 = (step // tiles) * SUBLANES
    t0 = (step % tiles) * TOK_TILE
    return [pltpu.make_async_copy(buf.at[slot, :, b, :], o_hbm.at[seq0 + b, pl.ds(t0, TOK_TILE), :],
                                  sem.at[slot])
            for b in range(SUBLANES)]


def _gathered_tile(x_hbm, buf, sem, seq_len, n_steps):
    step = _linear_step()
    slot = step % 2

    @pl.when(step == 0)
    def _():
        for c in _gather_copies(x_hbm, buf, sem, slot, step, seq_len):
            c.start()

    @pl.when(step + 1 < n_steps)
    def _():
        for c in _gather_copies(x_hbm, buf, sem, 1 - slot, step + 1, seq_len):
            c.start()

    for c in _gather_copies(x_hbm, buf, sem, slot, step, seq_len):
        c.wait()
    return buf[slot]


def _scattered_tile(write_rows, buf, o_hbm, sem, seq_len, n_steps):
    step = _linear_step()
    slot = step % 2

    @pl.when(step >= 2)
    def _():
        for c in _scatter_copies(buf, o_hbm, sem, slot, step - 2, seq_len):
            c.wait()

    write_rows(buf.at[slot])
    for c in _scatter_copies(buf, o_hbm, sem, slot, step, seq_len):
        c.start()

    @pl.when(step == n_steps - 1)
    def _():
        if n_steps > 1:
            for c in _scatter_copies(buf, o_hbm, sem, 1 - slot, step - 1, seq_len):
                c.wait()
        for c in _scatter_copies(buf, o_hbm, sem, slot, step, seq_len):
            c.wait()


def _conv3_tile(full, first, last, w, b):
    n = full.shape[0]
    zero = jnp.zeros((SUBLANES, full.shape[1]), _F32)
    return _conv3_rows(full[SUBLANES:n - SUBLANES], jnp.where(first, zero, full[:SUBLANES]),
                       jnp.where(last, zero, full[n - SUBLANES:]), w, b)


def _conv_mixer_kernel(gather, *refs):
    if gather:
        x_hbm, mod_ref, win_ref, cw_ref, cb_ref, wout_ref, g_ref, b_ref, o_ref, buf, sem = refs
        xa = _gathered_tile(x_hbm, buf, sem, *gather)
    else:
        (x_ref, xp_ref, xn_ref, mod_ref, win_ref, cw_ref, cb_ref, wout_ref, g_ref, b_ref,
         o_ref) = refs
        xa = jnp.concatenate([xp_ref[...], x_ref[...], xn_ref[...]], axis=0)
    t = pl.program_id(1)
    first = t == 0
    last = t == pl.num_programs(1) - 1
    sh = mod_ref[0]
    sc = 1.0 + mod_ref[1]
    gate = 1.0 + mod_ref[2]
    x = xa[1:xa.shape[0] - 1]
    u = _modulated_rows(xa, sc, sh)
    p = jnp.dot(u, win_ref[...], preferred_element_type=_F32)
    q = p[:, D_MODEL:2 * D_MODEL] * p[:, 2 * D_MODEL:]
    cq = _conv3_tile(q, first, last, cw_ref[...], cb_ref[...])
    r = (p[SUBLANES:p.shape[0] - SUBLANES, :D_MODEL] * cq).astype(_BF)
    tok = x.shape[0] // ROW_SPLITS
    rows = tok * SUBLANES
    for i in range(ROW_SPLITS):
        xr = x[i * tok:(i + 1) * tok]
        m = jnp.dot(r[i * rows:(i + 1) * rows], wout_ref[...], preferred_element_type=_F32)
        h = DEEPNORM_ALPHA * xr + gate[None] * m.reshape(xr.shape)
        o_ref[i * tok:(i + 1) * tok] = _layer_norm(h, g_ref[...], b_ref[...])


def _conv_mixer_layer(x, mod, w_in, conv_w, conv_b, w_out, ln_g, ln_b, seq_major_in=False):
    if seq_major_in:
        bsz, seq_len, _ = x.shape
    else:
        seq_len, bsz, _ = x.shape
    grid = (bsz // SUBLANES, seq_len // TOK_TILE)
    main, prev, nxt = _tile_specs(seq_len)
    weights = [_const_spec((D_MODEL, 3 * D_MODEL)), _const_spec((3, D_MODEL)),
               _const_spec((1, D_MODEL)), _const_spec((D_MODEL, D_MODEL)),
               _const_spec((1, D_MODEL)), _const_spec((1, D_MODEL))]
    if seq_major_in:
        gather = (seq_len, grid[0] * grid[1])
        x_specs, x_args = [pl.BlockSpec(memory_space=pl.ANY)], (x,)
        scratch = [pltpu.VMEM((2, TOK_TILE + 2, SUBLANES, D_MODEL), _F32),
                   pltpu.SemaphoreType.DMA((2,))]
        sem = ("arbitrary", "arbitrary")
    else:
        gather = None
        x_specs, x_args = [main, prev, nxt], (x, x, x)
        scratch = []
        sem = ("parallel", "parallel")
    return pl.pallas_call(
        functools.partial(_conv_mixer_kernel, gather),
        grid=grid,
        in_specs=x_specs + [_mod_spec()] + weights,
        out_specs=main,
        out_shape=jax.ShapeDtypeStruct((seq_len, bsz, D_MODEL), _F32),
        scratch_shapes=scratch,
        compiler_params=_params(*sem),
        name="conv_mixer_layer",
    )(*x_args, mod, w_in.astype(_BF), conv_w, conv_b.reshape(1, D_MODEL), w_out.astype(_BF),
      ln_g.reshape(1, D_MODEL), ln_b.reshape(1, D_MODEL))


def _ffn_kernel(scatter, x_ref, xp_ref, xn_ref, mod_ref, wup_ref, cw_ref, cb_ref, wdn_ref, g_ref,
                b_ref, o_ref, *dma_scratch):
    t = pl.program_id(1)
    first = t == 0
    last = t == pl.num_programs(1) - 1
    sh = mod_ref[3]
    sc = 1.0 + mod_ref[4]
    gate = 1.0 + mod_ref[5]
    x = x_ref[...]
    u = _modulated_rows(jnp.concatenate([xp_ref[...], x, xn_ref[...]], axis=0), sc, sh)
    tok = x.shape[0] // ROW_SPLITS
    rows = tok * SUBLANES
    accs = [None] * ROW_SPLITS
    k0 = 0
    for kc in FFN_COL_CHUNKS:
        halves = []
        for base in (0, FFN_HIDDEN):
            lo = base + k0
            a = jnp.dot(u, wup_ref[:, lo:lo + kc], preferred_element_type=_F32)
            halves.append(_conv3_tile(a, first, last, cw_ref[:, lo:lo + kc], cb_ref[:, lo:lo + kc]))
        hid = _gelu_times(halves[0], halves[1]).astype(_BF)
        for r in range(ROW_SPLITS):
            part = jnp.dot(hid[r * rows:(r + 1) * rows], wdn_ref[k0:k0 + kc, :],
                           preferred_element_type=_F32)
            accs[r] = part if accs[r] is None else accs[r] + part
        k0 += kc
    def write_rows(dst_ref):
        for r in range(ROW_SPLITS):
            xr = x[r * tok:(r + 1) * tok]
            h = DEEPNORM_ALPHA * xr + gate[None] * accs[r].reshape(xr.shape)
            dst_ref[r * tok:(r + 1) * tok] = _layer_norm(h, g_ref[...], b_ref[...])

    if scatter:
        _scattered_tile(write_rows, dma_scratch[0], o_ref, dma_scratch[1], *scatter)
    else:
        write_rows(o_ref)


def _ffn_layer(xt, mod, w_up, conv_w, conv_b, w_down, ln_g, ln_b, seq_major_out=False):
    seq_len, bsz, _ = xt.shape
    grid = (bsz // SUBLANES, seq_len // TOK_TILE)
    main, prev, nxt = _tile_specs(seq_len)
    if seq_major_out:
        scatter = (seq_len, grid[0] * grid[1])
        out_spec = pl.BlockSpec(memory_space=pl.ANY)
        out_shape = (bsz, seq_len, D_MODEL)
        scratch = [pltpu.VMEM((2, TOK_TILE, SUBLANES, D_MODEL), _F32), pltpu.SemaphoreType.DMA((2,))]
        sem = ("arbitrary", "arbitrary")
    else:
        scatter = None
        out_spec, out_shape, scratch = main, xt.shape, []
        sem = ("parallel", "parallel")
    return pl.pallas_call(
        functools.partial(_ffn_kernel, scatter),
        grid=grid,
        in_specs=[main, prev, nxt, _mod_spec(),
                  _const_spec((D_MODEL, 2 * FFN_HIDDEN)), _const_spec((3, 2 * FFN_HIDDEN)),
                  _const_spec((1, 2 * FFN_HIDDEN)), _const_spec((FFN_HIDDEN, D_MODEL)),
                  _const_spec((1, D_MODEL)), _const_spec((1, D_MODEL))],
        out_specs=out_spec,
        out_shape=jax.ShapeDtypeStruct(out_shape, _F32),
        scratch_shapes=scratch,
        compiler_params=_params(*sem),
        name="conv_ffn_layer",
    )(xt, xt, xt, mod, w_up.astype(_BF), conv_w, conv_b.reshape(1, 2 * FFN_HIDDEN),
      w_down.astype(_BF), ln_g.reshape(1, D_MODEL), ln_b.reshape(1, D_MODEL))


def _s5_tables(a_re, a_im, log_dt, b_re, b_im, c_re, c_im):
    n_lay, n_dir, n_grp, n_st = a_re.shape
    both = n_dir * n_st
    rows = n_lay * n_grp

    def states_last(v):
        return v.transpose(0, 2, 3, 1, 4).reshape(rows, v.shape[3], both)

    ldt = jnp.broadcast_to(log_dt[..., None], a_re.shape)
    par = states_last(jnp.stack([a_re, a_im, ldt], axis=3))
    par_col = par.transpose(0, 2, 1)
    b_col = jnp.stack([b_re, b_im], axis=2)
    b_col = b_col.transpose(0, 3, 2, 1, 4, 5).reshape(rows, 2, both, SSM_GROUP)
    b_row = jnp.stack([states_last(b_re.transpose(0, 1, 2, 4, 3)),
                       states_last(b_im.transpose(0, 1, 2, 4, 3))], axis=1)
    c_row = jnp.stack([states_last(c_re), states_last(c_im)], axis=1)
    blk = lambda *shape: pl.BlockSpec((None,) + shape, lambda g: (g,) + (0,) * len(shape))
    sq = (CHUNK_COLS, CHUNK_COLS)
    m_t, bt, ct_t, decay = pl.pallas_call(
        _s5_table_kernel,
        grid=(rows,),
        in_specs=[blk(3, both), blk(both, 3), blk(2, both, SSM_GROUP), blk(2, SSM_GROUP, both),
                  blk(2, SSM_GROUP, both)],
        out_specs=[blk(*sq), blk(*sq), blk(*sq), blk(2, both)],
        out_shape=[jax.ShapeDtypeStruct((rows,) + sq, _BF)] * 3
        + [jax.ShapeDtypeStruct((rows, 2, both), _F32)],
        compiler_params=_params("parallel"),
        name="s5_tables",
    )(par, par_col, b_col, b_row, c_row)
    per_layer = lambda v: v.reshape((n_lay, n_grp) + v.shape[1:])
    return [tuple(per_layer(v)[j] for v in (m_t, bt, ct_t, decay)) for j in range(n_lay)]


def _s5_table_kernel(par_ref, parc_ref, bcol_ref, brow_ref, crow_ref, mt_ref, bt_ref, ct_ref,
                     dec_ref):
    tt = CHUNK
    half = SSM_STATE
    both = 2 * half

    def discretise(a_re, a_im, log_dt):
        dt = jnp.exp(log_dt)
        lr, li = a_re * dt, a_im * dt

        def power(n):
            mag = jnp.exp(lr * n)
            return mag * jnp.cos(li * n), mag * jnp.sin(li * n)

        lb_re, lb_im = power(1.0)
        den = a_re * a_re + a_im * a_im
        f_re = ((lb_re - 1.0) * a_re + lb_im * a_im) / den
        f_im = (lb_im * a_re - (lb_re - 1.0) * a_im) / den
        return power, f_re, f_im

    power, f_re, f_im = discretise(par_ref[0:1, :], par_ref[1:2, :], par_ref[2:3, :])
    tok = lax.broadcasted_iota(jnp.int32, (tt, both), 0).astype(_F32)
    fwd = lax.broadcasted_iota(jnp.int32, (tt, both), 1) < half
    p_re, p_im = power(jnp.where(fwd, tt - 1.0 - tok, tok))
    w_re = (p_re * f_re - p_im * f_im)[:, None, :]
    w_im = (p_re * f_im + p_im * f_re)[:, None, :]
    b_re, b_im = brow_ref[0][None], brow_ref[1][None]
    bt_ref[:, 0:both] = (w_re * b_re - w_im * b_im).reshape(CHUNK_COLS, both).astype(_BF)
    bt_ref[:, both:] = (w_re * b_im + w_im * b_re).reshape(CHUNK_COLS, both).astype(_BF)
    e_re, e_im = power(jnp.where(fwd, tok + 1.0, tt - tok))
    e_re, e_im = e_re[:, None, :], e_im[:, None, :]
    c_re, c_im = crow_ref[0], crow_ref[1]
    ct_ref[:, 0:both] = (c_re[None] * e_re - c_im[None] * e_im).reshape(CHUNK_COLS, both).astype(_BF)
    ct_ref[:, both:] = (-(c_re[None] * e_im + c_im[None] * e_re)).reshape(CHUNK_COLS, both).astype(_BF)
    d_re, d_im = power(float(tt))
    dec_ref[0:1, :] = d_re
    dec_ref[1:2, :] = d_im

    n_s = 2 * tt
    power, f_re, f_im = discretise(parc_ref[:, 0:1], parc_ref[:, 1:2], parc_ref[:, 2:3])
    s_idx = lax.broadcasted_iota(jnp.int32, (both, n_s), 1).astype(_F32)
    fwd = lax.broadcasted_iota(jnp.int32, (both, n_s), 0) < half
    expo = jnp.where(fwd, tt - 1.0 - s_idx, s_idx - (tt - 1.0))
    live = expo >= 0.0
    p_re, p_im = power(jnp.maximum(expo, 0.0))
    w_re = jnp.where(live, p_re * f_re - p_im * f_im, 0.0)
    w_im = jnp.where(live, p_re * f_im + p_im * f_re, 0.0)
    width = n_s * SSM_GROUP
    lane = lax.broadcasted_iota(jnp.int32, (n_s, width), 1)
    spread_s = (lane // SSM_GROUP == lax.broadcasted_iota(jnp.int32, (n_s, width), 0)).astype(_F32)
    lane = lax.broadcasted_iota(jnp.int32, (SSM_GROUP, width), 1)
    spread_j = (lane % SSM_GROUP == lax.broadcasted_iota(jnp.int32, (SSM_GROUP, width), 0)).astype(_F32)
    dot = functools.partial(jnp.dot, precision=_HI, preferred_element_type=_F32)
    w_re, w_im = dot(w_re, spread_s), dot(w_im, spread_s)
    b_re, b_im = dot(bcol_ref[0], spread_j), dot(bcol_ref[1], spread_j)
    strip = dot(c_re, w_re * b_re - w_im * b_im) - dot(c_im, w_re * b_im + w_im * b_re)
    for t in range(tt):
        lo = SSM_GROUP * (tt - 1 - t)
        mt_ref[SSM_GROUP * t:SSM_GROUP * (t + 1), :] = strip[:, lo:lo + CHUNK_COLS].astype(_BF)


def _s5_regroup_kernel(x_ref, mod_ref, o_ref):
    sh = mod_ref[0]
    sc = 1.0 + mod_ref[1]
    cols = x_ref.shape[0] * x_ref.shape[2]
    for tau in range(CHUNK):
        slab = (x_ref[:, tau] * sc[None] + sh[None]).reshape(cols, D_MODEL)
        o_ref[:, SSM_GROUP * tau:SSM_GROUP * (tau + 1), :] = (
            slab.T.astype(_BF).reshape(SSM_GROUPS, SSM_GROUP, cols))


def _s5_chunk_kernel(bsz, n_chunks, z_ref, mt_ref, bt_ref, ct_ref, dec_ref, y_ref,
                     loc_ref, carry_ref):
    half = SSM_STATE
    n_grp = z_ref.shape[0]
    for g in range(n_grp):
        loc_ref[g] = lax.dot_general(z_ref[g], bt_ref[g], (((0,), (0,)), ((), ())),
                                     preferred_element_type=_F32)
    a_re = [jnp.broadcast_to(dec_ref[g, 0:1, :], (bsz, 2 * half)) for g in range(n_grp)]
    a_im = [jnp.broadcast_to(dec_ref[g, 1:2, :], (bsz, 2 * half)) for g in range(n_grp)]
    is_fwd = lax.broadcasted_iota(jnp.int32, (bsz, 2 * half), 1) < half

    def step(k, state):
        rf = pl.multiple_of(k * bsz, SUBLANES)
        rb = pl.multiple_of((n_chunks - 1 - k) * bsz, SUBLANES)
        new = []
        for g in range(n_grp):
            s_re, s_im = state[2 * g], state[2 * g + 1]
            carry_ref[g, pl.ds(rf, bsz), 0:half] = s_re[:, 0:half]
            carry_ref[g, pl.ds(rb, bsz), half:2 * half] = s_re[:, half:]
            carry_ref[g, pl.ds(rf, bsz), 2 * half:3 * half] = s_im[:, 0:half]
            carry_ref[g, pl.ds(rb, bsz), 3 * half:] = s_im[:, half:]
            l_re = jnp.where(is_fwd, loc_ref[g, pl.ds(rf, bsz), 0:2 * half],
                             loc_ref[g, pl.ds(rb, bsz), 0:2 * half])
            l_im = jnp.where(is_fwd, loc_ref[g, pl.ds(rf, bsz), 2 * half:],
                             loc_ref[g, pl.ds(rb, bsz), 2 * half:])
            new.append(a_re[g] * s_re - a_im[g] * s_im + l_re)
            new.append(a_re[g] * s_im + a_im[g] * s_re + l_im)
        return tuple(new)

    zero = jnp.zeros((bsz, 2 * half), _F32)
    lax.fori_loop(0, n_chunks, step, (zero,) * (2 * n_grp), unroll=S5_SCAN_UNROLL)
    for g in range(n_grp):
        y = jnp.dot(mt_ref[g], z_ref[g], preferred_element_type=_F32)
        y = y + lax.dot_general(ct_ref[g], carry_ref[g].astype(_BF), (((1,), (1,)), ((), ())),
                                preferred_element_type=_F32)
        y_ref[g] = y.astype(_BF)


def _s5_glu_kernel(x_ref, yt_ref, mod_ref, d_ref, wglu_ref, g_ref, b_ref, o_ref):
    sh = mod_ref[0]
    sc = 1.0 + mod_ref[1]
    gate = 1.0 + mod_ref[2]
    ncb, ntok, bsz, _ = x_ref.shape
    cols = ncb * bsz
    for t0 in range(0, ntok, GLU_SUB):
        zs = []
        for tq in range(t0, t0 + GLU_SUB):
            yd = yt_ref[:, SSM_GROUP * tq:SSM_GROUP * (tq + 1), :].astype(_F32)
            yd = yd.reshape(D_MODEL, cols).T.reshape(ncb, bsz, D_MODEL)
            u = x_ref[:, tq] * sc[None] + sh[None]
            zs.append(_gelu(d_ref[...] * u + yd).reshape(cols, D_MODEL).astype(_BF))
        pz = jnp.dot(jnp.concatenate(zs, axis=0), wglu_ref[...], preferred_element_type=_F32)
        m = pz[:, :D_MODEL] * jax.nn.sigmoid(pz[:, D_MODEL:])
        for k in range(GLU_SUB):
            mt = m[cols * k:cols * (k + 1)].reshape(ncb, bsz, D_MODEL)
            h = DEEPNORM_ALPHA * x_ref[:, t0 + k] + gate[None] * mt
            o_ref[:, t0 + k] = _layer_norm(h, g_ref[...], b_ref[...])


def _s5_mixer_layer(xt, mod, tables, d_skip, w_glu, ln_g, ln_b):
    seq_len, bsz, _ = xt.shape
    n_chunks = seq_len // CHUNK
    ncb = LANES // bsz
    cols = n_chunks * bsz
    x4 = xt.reshape(n_chunks, CHUNK, bsz, D_MODEL)
    mod_full = pl.BlockSpec((6, bsz, D_MODEL), lambda *_: (0, 0, 0))
    zt = pl.pallas_call(
        _s5_regroup_kernel,
        grid=(n_chunks // ncb,),
        in_specs=[pl.BlockSpec((ncb, CHUNK, bsz, D_MODEL), lambda i: (i, 0, 0, 0)), mod_full],
        out_specs=pl.BlockSpec((SSM_GROUPS, CHUNK_COLS, LANES), lambda i: (0, 0, i)),
        out_shape=jax.ShapeDtypeStruct((SSM_GROUPS, CHUNK_COLS, cols), _BF),
        compiler_params=_params("parallel"),
        name="s5_regroup",
    )(x4, mod)
    gs = S5_GROUPS_PER_STEP
    grp = lambda r, c: pl.BlockSpec((gs, r, c), lambda g: (g, 0, 0))
    m_t, bt, ct_t, decay = tables
    yt = pl.pallas_call(
        functools.partial(_s5_chunk_kernel, bsz, n_chunks),
        grid=(SSM_GROUPS // gs,),
        in_specs=[grp(CHUNK_COLS, cols), grp(CHUNK_COLS, CHUNK_COLS), grp(CHUNK_COLS, CHUNK_COLS),
                  grp(CHUNK_COLS, CHUNK_COLS), grp(2, 2 * SSM_STATE)],
        out_specs=grp(CHUNK_COLS, cols),
        out_shape=jax.ShapeDtypeStruct((SSM_GROUPS, CHUNK_COLS, cols), _BF),
        scratch_shapes=[pltpu.VMEM((gs, cols, CHUNK_COLS), _F32),
                        pltpu.VMEM((gs, cols, CHUNK_COLS), _F32)],
        compiler_params=_params("parallel"),
        name="s5_chunk_scan",
    )(zt, m_t, bt, ct_t, decay)
    xblk = pl.BlockSpec((ncb, GLU_TOKS, bsz, D_MODEL), lambda i, q: (i, q, 0, 0))
    out = pl.pallas_call(
        _s5_glu_kernel,
        grid=(n_chunks // ncb, CHUNK // GLU_TOKS),
        in_specs=[xblk,
                  pl.BlockSpec((SSM_GROUPS, SSM_GROUP * GLU_TOKS, LANES), lambda i, q: (0, q, i)),
                  mod_full, _const_spec((1, D_MODEL)), _const_spec((D_MODEL, 2 * D_MODEL)),
                  _const_spec((1, D_MODEL)), _const_spec((1, D_MODEL))],
        out_specs=xblk,
        out_shape=jax.ShapeDtypeStruct(x4.shape, _F32),
        compiler_params=_params("parallel", "parallel"),
        name="s5_glu_layer",
    )(x4, yt, mod, d_skip.reshape(1, D_MODEL), w_glu.astype(_BF),
      ln_g.reshape(1, D_MODEL), ln_b.reshape(1, D_MODEL))
    return out.reshape(xt.shape)


def _trunk(x, mod, p, s5_tables):
    xt = x
    for i in range(DEPTH):
        j = i // 2
        if i % 2 == 0:
            xt = _conv_mixer_layer(xt, mod[i], p["sc_w_in"][j], p["sc_conv_w"][j],
                                   p["sc_conv_b"][j], p["sc_w_out"][j], p["ln1_g"][i], p["ln1_b"][i],
                                   seq_major_in=(i == 0))
        else:
            xt = _s5_mixer_layer(xt, mod[i], s5_tables[j], p["s5_d"][j], p["s5_w_glu"][j],
                                 p["ln1_g"][i], p["ln1_b"][i])
        xt = _ffn_layer(xt, mod[i], p["ffn_w_up"][i], p["ffn_conv_w"][i], p["ffn_conv_b"][i],
                        p["ffn_w_down"][i], p["ln2_g"][i], p["ln2_b"][i],
                        seq_major_out=(i == DEPTH - 1))
    return xt


def kernel(x_prompt, x_sample, c_prompt, c_sample, ada_w, ada_b, ln1_g, ln1_b, ln2_g, ln2_b, sc_w_in, sc_conv_w, sc_conv_b, sc_w_out, s5_a_re, s5_a_im, s5_log_dt, s5_b_re, s5_b_im, s5_c_re, s5_c_im, s5_d, s5_w_glu, ffn_w_up, ffn_conv_w, ffn_conv_b, ffn_w_down):
    p = dict(ada_w=ada_w, ada_b=ada_b, ln1_g=ln1_g, ln1_b=ln1_b, ln2_g=ln2_g, ln2_b=ln2_b,
             sc_w_in=sc_w_in, sc_conv_w=sc_conv_w, sc_conv_b=sc_conv_b, sc_w_out=sc_w_out,
             s5_d=s5_d, s5_w_glu=s5_w_glu, ffn_w_up=ffn_w_up, ffn_conv_w=ffn_conv_w,
             ffn_conv_b=ffn_conv_b, ffn_w_down=ffn_w_down)
    s5_tables = _s5_tables(s5_a_re, s5_a_im, s5_log_dt, s5_b_re, s5_b_im, s5_c_re, s5_c_im)
    n_prompt = c_prompt.shape[0]
    mod = _ada(jnp.concatenate([c_prompt, c_sample], axis=0), ada_w, ada_b)
    y_prompt = _trunk(x_prompt, mod[:, :, :n_prompt], p, s5_tables)
    y_sample = _trunk(x_sample, mod[:, :, n_prompt:], p, s5_tables)
    return (y_prompt, y_sample)
```

```python
import functools
import math

import jax
import jax.numpy as jnp
from jax import lax
from jax.experimental import pallas as pl
from jax.experimental.pallas import tpu as pltpu

D_MODEL = 1024
DEPTH = 4
FFN_HIDDEN = 2816
SSM_GROUP = 16
SSM_GROUPS = D_MODEL // SSM_GROUP
SSM_STATE = 64
DEEPNORM_ALPHA = (2 * DEPTH) ** 0.25
LN_EPS = 1e-5

SUBLANES = 8
LANES = 128
CHUNK = 16
CHUNK_COLS = CHUNK * SSM_GROUP
assert CHUNK & (CHUNK - 1) == 0
TOK_TILE = 128
GLU_TOKS = 16
GLU_SUB = 2
S5_GROUPS_PER_STEP = 2
S5_SCAN_UNROLL = 2
FFN_COL_CHUNKS = (1280, 1536)
ROW_SPLITS = 4
VMEM_LIMIT = 56 * 1024 * 1024

_HI = lax.Precision.HIGHEST
_BF = jnp.bfloat16
_F32 = jnp.float32


def _const_spec(shape):
    nd = len(shape)
    return pl.BlockSpec(shape, lambda *_: (0,) * nd, pipeline_mode=pl.Buffered(1))


def _params(*sem):
    return pltpu.CompilerParams(dimension_semantics=sem, vmem_limit_bytes=VMEM_LIMIT)


_GELU_C1 = math.sqrt(2.0 / math.pi)
_GELU_C2 = _GELU_C1 * 0.044715


def _gelu_half_gate(x):
    return 0.5 + 0.5 * jnp.tanh(x * (_GELU_C1 + _GELU_C2 * (x * x)))


def _gelu(x):
    return x * _gelu_half_gate(x)


def _gelu_times(x, v):
    return (x * v) * _gelu_half_gate(x)


def _layer_norm(h, g, b):
    mu = jnp.mean(h, axis=-1, keepdims=True)
    d = h - mu
    var = jnp.mean(d * d, axis=-1, keepdims=True)
    return d * lax.rsqrt(var + LN_EPS) * g + b


def _conv3_rows(cur, before, after, w, b):
    up = jnp.concatenate([before, cur[:-SUBLANES]], axis=0)
    dn = jnp.concatenate([cur[SUBLANES:], after], axis=0)
    return up * w[0:1, :] + cur * w[1:2, :] + dn * w[2:3, :] + b


def _ada_kernel(c_ref, w_ref, b_ref, o_ref):
    c = c_ref[...]
    ca = c * jax.nn.sigmoid(c)
    o_ref[...] = jnp.dot(ca, w_ref[...], precision=_HI, preferred_element_type=_F32) + b_ref[...]


def _ada(c, ada_w, ada_b):
    bsz = c.shape[0]
    return pl.pallas_call(
        _ada_kernel,
        grid=(DEPTH, 6),
        in_specs=[
            pl.BlockSpec((bsz, D_MODEL), lambda i, n: (0, 0)),
            pl.BlockSpec((None, D_MODEL, D_MODEL), lambda i, n: (i, 0, n)),
            pl.BlockSpec((None, 1, D_MODEL), lambda i, n: (i, 0, n)),
        ],
        out_specs=pl.BlockSpec((None, None, bsz, D_MODEL), lambda i, n: (i, n, 0, 0)),
        out_shape=jax.ShapeDtypeStruct((DEPTH, 6, bsz, D_MODEL), _F32),
        compiler_params=_params("arbitrary", "arbitrary"),
        name="ada_mod",
    )(c, ada_w, ada_b.reshape(DEPTH, 1, 6 * D_MODEL))


def _tile_specs(seq_len):
    main = pl.BlockSpec((TOK_TILE, SUBLANES, D_MODEL), lambda b, t: (t, b, 0))
    prev = pl.BlockSpec((1, SUBLANES, D_MODEL),
                        lambda b, t: (jnp.maximum(t * TOK_TILE - 1, 0), b, 0))
    nxt = pl.BlockSpec((1, SUBLANES, D_MODEL),
                       lambda b, t: (jnp.minimum((t + 1) * TOK_TILE, seq_len - 1), b, 0))
    return main, prev, nxt


def _mod_spec():
    return pl.BlockSpec((6, SUBLANES, D_MODEL), lambda b, t: (0, b, 0))


def _modulated_rows(xa, sc, sh):
    u = xa * sc[None] + sh[None]
    return u.reshape(xa.shape[0] * SUBLANES, D_MODEL).astype(_BF)


def _linear_step():
    return pl.program_id(0) * pl.num_programs(1) + pl.program_id(1)


def _gather_copies(x_hbm, buf, sem, slot, step, seq_len):
    tiles = seq_len // TOK_TILE
    seq0 = (step // tiles) * SUBLANES
    t0 = (step % tiles) * TOK_TILE
    spans = ((jnp.maximum(t0 - 1, 0), 1, 0), (t0, TOK_TILE, 1),
             (jnp.minimum(t0 + TOK_TILE, seq_len - 1), 1, TOK_TILE + 1))
    return [pltpu.make_async_copy(x_hbm.at[seq0 + b, pl.ds(src, n), :],
                                  buf.at[slot, pl.ds(dst, n), b, :], sem.at[slot])
            for b in range(SUBLANES) for src, n, dst in spans]


def _scatter_copies(buf, o_hbm, sem, slot, step, seq_len):
    tiles = seq_len // TOK_TILE
    seq0 = (step // tiles) * SUBLANES
    t0 = (step % tiles) * TOK_TILE
    return [pltpu.make_async_copy(buf.at[slot, :, b, :], o_hbm.at[seq0 + b, pl.ds(t0, TOK_TILE), :],
                                  sem.at[slot])
            for b in range(SUBLANES)]


def _gathered_tile(x_hbm, buf, sem, seq_len, n_steps):
    step = _linear_step()
    slot = step % 2

    @pl.when(step == 0)
    def _():
        for c in _gather_copies(x_hbm, buf, sem, slot, step, seq_len):
            c.start()

    @pl.when(step + 1 < n_steps)
    def _():
        for c in _gather_copies(x_hbm, buf, sem, 1 - slot, step + 1, seq_len):
            c.start()

    for c in _gather_copies(x_hbm, buf, sem, slot, step, seq_len):
        c.wait()
    return buf[slot]


def _scatter_slot(buf, o_hbm, sem, seq_len):
    step = _linear_step()
    slot = step % 2

    @pl.when(step >= 2)
    def _():
        for c in _scatter_copies(buf, o_hbm, sem, slot, step - 2, seq_len):
            c.wait()

    return buf.at[slot]


def _scatter_start(buf, o_hbm, sem, seq_len, n_steps):
    step = _linear_step()
    slot = step % 2
    for c in _scatter_copies(buf, o_hbm, sem, slot, step, seq_len):
        c.start()

    @pl.when(step == n_steps - 1)
    def _():
        if n_steps > 1:
            for c in _scatter_copies(buf, o_hbm, sem, 1 - slot, step - 1, seq_len):
                c.wait()
        for c in _scatter_copies(buf, o_hbm, sem, slot, step, seq_len):
            c.wait()


def _conv3_tile(full, first, last, w, b):
    n = full.shape[0]
    zero = jnp.zeros((SUBLANES, full.shape[1]), _F32)
    return _conv3_rows(full[SUBLANES:n - SUBLANES], jnp.where(first, zero, full[:SUBLANES]),
                       jnp.where(last, zero, full[n - SUBLANES:]), w, b)


def _conv_mixer_kernel(gather, *refs):
    if gather:
        x_hbm, mod_ref, win_ref, cw_ref, cb_ref, wout_ref, g_ref, b_ref, o_ref, buf, sem = refs
        xa = _gathered_tile(x_hbm, buf, sem, *gather)
    else:
        (x_ref, xp_ref, xn_ref, mod_ref, win_ref, cw_ref, cb_ref, wout_ref, g_ref, b_ref,
         o_ref) = refs
        xa = jnp.concatenate([xp_ref[...], x_ref[...], xn_ref[...]], axis=0)
    t = pl.program_id(1)
    first = t == 0
    last = t == pl.num_programs(1) - 1
    sh = mod_ref[0]
    sc = 1.0 + mod_ref[1]
    gate = 1.0 + mod_ref[2]
    x = xa[1:xa.shape[0] - 1]
    u = _modulated_rows(xa, sc, sh)
    p = jnp.dot(u, win_ref[...], preferred_element_type=_F32)
    q = p[:, D_MODEL:2 * D_MODEL] * p[:, 2 * D_MODEL:]
    cq = _conv3_tile(q, first, last, cw_ref[...], cb_ref[...])
    r = (p[SUBLANES:p.shape[0] - SUBLANES, :D_MODEL] * cq).astype(_BF)
    tok = x.shape[0] // ROW_SPLITS
    rows = tok * SUBLANES
    for i in range(ROW_SPLITS):
        xr = x[i * tok:(i + 1) * tok]
        m = jnp.dot(r[i * rows:(i + 1) * rows], wout_ref[...], preferred_element_type=_F32)
        h = DEEPNORM_ALPHA * xr + gate[None] * m.reshape(xr.shape)
        o_ref[i * tok:(i + 1) * tok] = _layer_norm(h, g_ref[...], b_ref[...])


def _conv_mixer_layer(x, mod, w_in, conv_w, conv_b, w_out, ln_g, ln_b, seq_major_in=False):
    if seq_major_in:
        bsz, seq_len, _ = x.shape
    else:
        seq_len, bsz, _ = x.shape
    grid = (bsz // SUBLANES, seq_len // TOK_TILE)
    main, prev, nxt = _tile_specs(seq_len)
    weights = [_const_spec((D_MODEL, 3 * D_MODEL)), _const_spec((3, D_MODEL)),
               _const_spec((1, D_MODEL)), _const_spec((D_MODEL, D_MODEL)),
               _const_spec((1, D_MODEL)), _const_spec((1, D_MODEL))]
    if seq_major_in:
        gather = (seq_len, grid[0] * grid[1])
        x_specs, x_args = [pl.BlockSpec(memory_space=pl.ANY)], (x,)
        scratch = [pltpu.VMEM((2, TOK_TILE + 2, SUBLANES, D_MODEL), _F32),
                   pltpu.SemaphoreType.DMA((2,))]
        sem = ("arbitrary", "arbitrary")
    else:
        gather = None
        x_specs, x_args = [main, prev, nxt], (x, x, x)
        scratch = []
        sem = ("parallel", "parallel")
    return pl.pallas_call(
        functools.partial(_conv_mixer_kernel, gather),
        grid=grid,
        in_specs=x_specs + [_mod_spec()] + weights,
        out_specs=main,
        out_shape=jax.ShapeDtypeStruct((seq_len, bsz, D_MODEL), _F32),
        scratch_shapes=scratch,
        compiler_params=_params(*sem),
        name="conv_mixer_layer",
    )(*x_args, mod, w_in.astype(_BF), conv_w, conv_b.reshape(1, D_MODEL), w_out.astype(_BF),
      ln_g.reshape(1, D_MODEL), ln_b.reshape(1, D_MODEL))


def _ffn_kernel(scatter, x_ref, xp_ref, xn_ref, mod_ref, wup_ref, cw_ref, cb_ref, wdn_ref, g_ref,
                b_ref, o_ref, *dma_scratch):
    dst_ref = _scatter_slot(dma_scratch[0], o_ref, dma_scratch[1], scatter[0]) if scatter else o_ref
    t = pl.program_id(1)
    first = t == 0
    last = t == pl.num_programs(1) - 1
    sh = mod_ref[3]
    sc = 1.0 + mod_ref[4]
    gate = 1.0 + mod_ref[5]
    x = x_ref[...]
    u = _modulated_rows(jnp.concatenate([xp_ref[...], x, xn_ref[...]], axis=0), sc, sh)
    tok = x.shape[0] // ROW_SPLITS
    rows = tok * SUBLANES
    accs = [None] * ROW_SPLITS
    k0 = 0
    for kc in FFN_COL_CHUNKS:
        halves = []
        for base in (0, FFN_HIDDEN):
            lo = base + k0
            a = jnp.dot(u, wup_ref[:, lo:lo + kc], preferred_element_type=_F32)
            halves.append(_conv3_tile(a, first, last, cw_ref[:, lo:lo + kc], cb_ref[:, lo:lo + kc]))
        hid = _gelu_times(halves[0], halves[1]).astype(_BF)
        for r in range(ROW_SPLITS):
            part = jnp.dot(hid[r * rows:(r + 1) * rows], wdn_ref[k0:k0 + kc, :],
                           preferred_element_type=_F32)
            accs[r] = part if accs[r] is None else accs[r] + part
        k0 += kc
    for r in range(ROW_SPLITS):
        xr = x[r * tok:(r + 1) * tok]
        h = DEEPNORM_ALPHA * xr + gate[None] * accs[r].reshape(xr.shape)
        dst_ref[r * tok:(r + 1) * tok] = _layer_norm(h, g_ref[...], b_ref[...])
    if scatter:
        _scatter_start(dma_scratch[0], o_ref, dma_scratch[1], *scatter)


def _ffn_layer(xt, mod, w_up, conv_w, conv_b, w_down, ln_g, ln_b, seq_major_out=False):
    seq_len, bsz, _ = xt.shape
    grid = (bsz // SUBLANES, seq_len // TOK_TILE)
    main, prev, nxt = _tile_specs(seq_len)
    if seq_major_out:
        scatter = (seq_len, grid[0] * grid[1])
        out_spec = pl.BlockSpec(memory_space=pl.ANY)
        out_shape = (bsz, seq_len, D_MODEL)
        scratch = [pltpu.VMEM((2, TOK_TILE, SUBLANES, D_MODEL), _F32), pltpu.SemaphoreType.DMA((2,))]
        sem = ("arbitrary", "arbitrary")
    else:
        scatter = None
        out_spec, out_shape, scratch = main, xt.shape, []
        sem = ("parallel", "parallel")
    return pl.pallas_call(
        functools.partial(_ffn_kernel, scatter),
        grid=grid,
        in_specs=[main, prev, nxt, _mod_spec(),
                  _const_spec((D_MODEL, 2 * FFN_HIDDEN)), _const_spec((3, 2 * FFN_HIDDEN)),
                  _const_spec((1, 2 * FFN_HIDDEN)), _const_spec((FFN_HIDDEN, D_MODEL)),
                  _const_spec((1, D_MODEL)), _const_spec((1, D_MODEL))],
        out_specs=out_spec,
        out_shape=jax.ShapeDtypeStruct(out_shape, _F32),
        scratch_shapes=scratch,
        compiler_params=_params(*sem),
        name="conv_ffn_layer",
    )(xt, xt, xt, mod, w_up.astype(_BF), conv_w, conv_b.reshape(1, 2 * FFN_HIDDEN),
      w_down.astype(_BF), ln_g.reshape(1, D_MODEL), ln_b.reshape(1, D_MODEL))


def _s5_tables(a_re, a_im, log_dt, b_re, b_im, c_re, c_im):
    n_lay, n_dir, n_grp, n_st = a_re.shape
    both = n_dir * n_st
    rows = n_lay * n_grp

    def states_last(v):
        return v.transpose(0, 2, 3, 1, 4).reshape(rows, v.shape[3], both)

    ldt = jnp.broadcast_to(log_dt[..., None], a_re.shape)
    par = states_last(jnp.stack([a_re, a_im, ldt], axis=3))
    b_col = jnp.stack([b_re, b_im], axis=2)
    b_col = b_col.transpose(0, 3, 2, 1, 4, 5).reshape(rows, 2, both, SSM_GROUP)
    b_row = jnp.stack([states_last(b_re.transpose(0, 1, 2, 4, 3)),
                       states_last(b_im.transpose(0, 1, 2, 4, 3))], axis=1)
    c_row = jnp.stack([states_last(c_re), states_last(c_im)], axis=1)
    blk = lambda *shape: pl.BlockSpec((None,) + shape, lambda g: (g,) + (0,) * len(shape))
    sq = (CHUNK_COLS, CHUNK_COLS)
    m_t, bt, ct_t, decay = pl.pallas_call(
        _s5_table_kernel,
        grid=(rows,),
        in_specs=[blk(3, both), blk(2, both, SSM_GROUP), blk(2, SSM_GROUP, both),
                  blk(2, SSM_GROUP, both)],
        out_specs=[blk(*sq), blk(*sq), blk(*sq), blk(2, both)],
        out_shape=[jax.ShapeDtypeStruct((rows,) + sq, _BF)] * 3
        + [jax.ShapeDtypeStruct((rows, 2, both), _F32)],
        compiler_params=_params("parallel"),
        name="s5_tables",
    )(par, b_col, b_row, c_row)
    per_layer = lambda v: v.reshape((n_lay, n_grp) + v.shape[1:])
    return [tuple(per_layer(v)[j] for v in (m_t, bt, ct_t, decay)) for j in range(n_lay)]


def _s5_table_kernel(par_ref, bcol_ref, brow_ref, crow_ref, mt_ref, bt_ref, ct_ref, dec_ref):
    tt = CHUNK
    half = SSM_STATE
    both = 2 * half

    def discretise(a_re, a_im, log_dt):
        dt = jnp.exp(log_dt)
        mag = jnp.exp(a_re * dt)
        lb_re, lb_im = mag * jnp.cos(a_im * dt), mag * jnp.sin(a_im * dt)
        den = a_re * a_re + a_im * a_im
        f_re = ((lb_re - 1.0) * a_re + lb_im * a_im) / den
        f_im = (lb_im * a_re - (lb_re - 1.0) * a_im) / den
        squares = [(lb_re, lb_im)]
        while len(squares) <= tt.bit_length() - 1:
            r, i = squares[-1]
            squares.append((r * r - i * i, 2.0 * (r * i)))

        def power(n):
            p_re = jnp.ones(n.shape, _F32)
            p_im = jnp.zeros(n.shape, _F32)
            for k, (s_re, s_im) in enumerate(squares):
                bit = ((n >> k) & 1) == 1
                p_re, p_im = (jnp.where(bit, p_re * s_re - p_im * s_im, p_re),
                              jnp.where(bit, p_re * s_im + p_im * s_re, p_im))
            return p_re, p_im

        return power, squares[-1], f_re, f_im

    power, lb_chunk, f_re, f_im = discretise(par_ref[0:1, :], par_ref[1:2, :], par_ref[2:3, :])
    tok = lax.broadcasted_iota(jnp.int32, (tt, both), 0)
    fwd = lax.broadcasted_iota(jnp.int32, (tt, both), 1) < half
    p_re, p_im = power(jnp.where(fwd, tt - 1 - tok, tok))
    w_re = (p_re * f_re - p_im * f_im)[:, None, :]
    w_im = (p_re * f_im + p_im * f_re)[:, None, :]
    b_re, b_im = brow_ref[0][None], brow_ref[1][None]
    bt_ref[:, 0:both] = (w_re * b_re - w_im * b_im).reshape(CHUNK_COLS, both).astype(_BF)
    bt_ref[:, both:] = (w_re * b_im + w_im * b_re).reshape(CHUNK_COLS, both).astype(_BF)
    e_re, e_im = power(jnp.where(fwd, tok + 1, tt - tok))
    e_re, e_im = e_re[:, None, :], e_im[:, None, :]
    c_re, c_im = crow_ref[0], crow_ref[1]
    ct_ref[:, 0:both] = (c_re[None] * e_re - c_im[None] * e_im).reshape(CHUNK_COLS, both).astype(_BF)
    ct_ref[:, both:] = (-(c_re[None] * e_im + c_im[None] * e_re)).reshape(CHUNK_COLS, both).astype(_BF)
    dec_ref[0:1, :] = lb_chunk[0]
    dec_ref[1:2, :] = lb_chunk[1]

    n_s = 2 * tt
    s_idx = lax.broadcasted_iota(jnp.int32, (n_s, both), 0)
    fwd = lax.broadcasted_iota(jnp.int32, (n_s, both), 1) < half
    expo = jnp.where(fwd, tt - 1 - s_idx, s_idx - (tt - 1))
    live = expo >= 0
    p_re, p_im = power(jnp.maximum(expo, 0))
    w_re = jnp.where(live, p_re * f_re - p_im * f_im, 0.0).T
    w_im = jnp.where(live, p_re * f_im + p_im * f_re, 0.0).T
    width = n_s * SSM_GROUP

    def spread(v, pick):
        k = v.shape[1]
        row = lax.broadcasted_iota(jnp.int32, (3 * k, width), 0)
        onehot = (pick(lax.broadcasted_iota(jnp.int32, (3 * k, width), 1)) == row % k).astype(_BF)
        hi = v.astype(_BF)
        rest = v - hi.astype(_F32)
        mid = rest.astype(_BF)
        lo = (rest - mid.astype(_F32)).astype(_BF)
        return jnp.dot(jnp.concatenate([hi, mid, lo], axis=1), onehot, preferred_element_type=_F32)

    def dot3(a, b):
        a_hi, b_hi = a.astype(_BF), b.astype(_BF)
        a_lo, b_lo = (a - a_hi.astype(_F32)).astype(_BF), (b - b_hi.astype(_F32)).astype(_BF)
        d = functools.partial(jnp.dot, preferred_element_type=_F32)
        return d(a_hi, b_hi) + (d(a_hi, b_lo) + d(a_lo, b_hi))

    w_re, w_im = (spread(w, lambda l: l // SSM_GROUP) for w in (w_re, w_im))
    b_re, b_im = (spread(bcol_ref[k], lambda l: l % SSM_GROUP) for k in (0, 1))
    strip = dot3(c_re, w_re * b_re - w_im * b_im) - dot3(c_im, w_re * b_im + w_im * b_re)
    for t in range(tt):
        lo = SSM_GROUP * (tt - 1 - t)
        mt_ref[SSM_GROUP * t:SSM_GROUP * (t + 1), :] = strip[:, lo:lo + CHUNK_COLS].astype(_BF)


def _s5_regroup_kernel(x_ref, mod_ref, o_ref):
    sh = mod_ref[0]
    sc = 1.0 + mod_ref[1]
    cols = x_ref.shape[0] * x_ref.shape[2]
    for tau in range(CHUNK):
        slab = (x_ref[:, tau] * sc[None] + sh[None]).reshape(cols, D_MODEL)
        o_ref[:, SSM_GROUP * tau:SSM_GROUP * (tau + 1), :] = (
            slab.T.astype(_BF).reshape(SSM_GROUPS, SSM_GROUP, cols))


def _s5_chunk_kernel(bsz, n_chunks, z_ref, mt_ref, bt_ref, ct_ref, dec_ref, y_ref,
                     loc_ref, carry_ref):
    half = SSM_STATE
    n_grp = z_ref.shape[0]
    for g in range(n_grp):
        loc_ref[g] = lax.dot_general(z_ref[g], bt_ref[g], (((0,), (0,)), ((), ())),
                                     preferred_element_type=_F32)
    a_re = [jnp.broadcast_to(dec_ref[g, 0:1, :], (bsz, 2 * half)) for g in range(n_grp)]
    a_im = [jnp.broadcast_to(dec_ref[g, 1:2, :], (bsz, 2 * half)) for g in range(n_grp)]
    is_fwd = lax.broadcasted_iota(jnp.int32, (bsz, 2 * half), 1) < half

    def step(k, state):
        rf = pl.multiple_of(k * bsz, SUBLANES)
        rb = pl.multiple_of((n_chunks - 1 - k) * bsz, SUBLANES)
        new = []
        for g in range(n_grp):
            s_re, s_im = state[2 * g], state[2 * g + 1]
            carry_ref[g, pl.ds(rf, bsz), 0:half] = s_re[:, 0:half]
            carry_ref[g, pl.ds(rb, bsz), half:2 * half] = s_re[:, half:]
            carry_ref[g, pl.ds(rf, bsz), 2 * half:3 * half] = s_im[:, 0:half]
            carry_ref[g, pl.ds(rb, bsz), 3 * half:] = s_im[:, half:]
            l_re = jnp.where(is_fwd, loc_ref[g, pl.ds(rf, bsz), 0:2 * half],
                             loc_ref[g, pl.ds(rb, bsz), 0:2 * half])
            l_im = jnp.where(is_fwd, loc_ref[g, pl.ds(rf, bsz), 2 * half:],
                             loc_ref[g, pl.ds(rb, bsz), 2 * half:])
            new.append(a_re[g] * s_re - a_im[g] * s_im + l_re)
            new.append(a_re[g] * s_im + a_im[g] * s_re + l_im)
        return tuple(new)

    zero = jnp.zeros((bsz, 2 * half), _F32)
    lax.fori_loop(0, n_chunks, step, (zero,) * (2 * n_grp), unroll=S5_SCAN_UNROLL)
    for g in range(n_grp):
        y = jnp.dot(mt_ref[g], z_ref[g], preferred_element_type=_F32)
        y = y + lax.dot_general(ct_ref[g], carry_ref[g].astype(_BF), (((1,), (1,)), ((), ())),
                                preferred_element_type=_F32)
        y_ref[g] = y.astype(_BF)


def _s5_glu_kernel(x_ref, yt_ref, mod_ref, d_ref, wglu_ref, g_ref, b_ref, o_ref):
    sh = mod_ref[0]
    sc = 1.0 + mod_ref[1]
    gate = 1.0 + mod_ref[2]
    ncb, ntok, bsz, _ = x_ref.shape
    cols = ncb * bsz
    for t0 in range(0, ntok, GLU_SUB):
        zs = []
        for tq in range(t0, t0 + GLU_SUB):
            yd = yt_ref[:, SSM_GROUP * tq:SSM_GROUP * (tq + 1), :].astype(_F32)
            yd = yd.reshape(D_MODEL, cols).T.reshape(ncb, bsz, D_MODEL)
            u = x_ref[:, tq] * sc[None] + sh[None]
            zs.append(_gelu(d_ref[...] * u + yd).reshape(cols, D_MODEL).astype(_BF))
        pz = jnp.dot(jnp.concatenate(zs, axis=0), wglu_ref[...], preferred_element_type=_F32)
        m = pz[:, :D_MODEL] * jax.nn.sigmoid(pz[:, D_MODEL:])
        for k in range(GLU_SUB):
            mt = m[cols * k:cols * (k + 1)].reshape(ncb, bsz, D_MODEL)
            h = DEEPNORM_ALPHA * x_ref[:, t0 + k] + gate[None] * mt
            o_ref[:, t0 + k] = _layer_norm(h, g_ref[...], b_ref[...])


def _s5_mixer_layer(xt, mod, tables, d_skip, w_glu, ln_g, ln_b):
    seq_len, bsz, _ = xt.shape
    n_chunks = seq_len // CHUNK
    ncb = LANES // bsz
    cols = n_chunks * bsz
    x4 = xt.reshape(n_chunks, CHUNK, bsz, D_MODEL)
    mod_full = pl.BlockSpec((6, bsz, D_MODEL), lambda *_: (0, 0, 0))
    zt = pl.pallas_call(
        _s5_regroup_kernel,
        grid=(n_chunks // ncb,),
        in_specs=[pl.BlockSpec((ncb, CHUNK, bsz, D_MODEL), lambda i: (i, 0, 0, 0)), mod_full],
        out_specs=pl.BlockSpec((SSM_GROUPS, CHUNK_COLS, LANES), lambda i: (0, 0, i)),
        out_shape=jax.ShapeDtypeStruct((SSM_GROUPS, CHUNK_COLS, cols), _BF),
        compiler_params=_params("parallel"),
        name="s5_regroup",
    )(x4, mod)
    gs = S5_GROUPS_PER_STEP
    grp = lambda r, c: pl.BlockSpec((gs, r, c), lambda g: (g, 0, 0))
    m_t, bt, ct_t, decay = tables
    yt = pl.pallas_call(
        functools.partial(_s5_chunk_kernel, bsz, n_chunks),
        grid=(SSM_GROUPS // gs,),
        in_specs=[grp(CHUNK_COLS, cols), grp(CHUNK_COLS, CHUNK_COLS), grp(CHUNK_COLS, CHUNK_COLS),
                  grp(CHUNK_COLS, CHUNK_COLS), grp(2, 2 * SSM_STATE)],
        out_specs=grp(CHUNK_COLS, cols),
        out_shape=jax.ShapeDtypeStruct((SSM_GROUPS, CHUNK_COLS, cols), _BF),
        scratch_shapes=[pltpu.VMEM((gs, cols, CHUNK_COLS), _F32),
                        pltpu.VMEM((gs, cols, CHUNK_COLS), _F32)],
        compiler_params=_params("parallel"),
        name="s5_chunk_scan",
    )(zt, m_t, bt, ct_t, decay)
    xblk = pl.BlockSpec((ncb, GLU_TOKS, bsz, D_MODEL), lambda i, q: (i, q, 0, 0))
    out = pl.pallas_call(
        _s5_glu_kernel,
        grid=(n_chunks // ncb, CHUNK // GLU_TOKS),
        in_specs=[xblk,
                  pl.BlockSpec((SSM_GROUPS, SSM_GROUP * GLU_TOKS, LANES), lambda i, q: (0, q, i)),
                  mod_full, _const_spec((1, D_MODEL)), _const_spec((D_MODEL, 2 * D_MODEL)),
                  _const_spec((1, D_MODEL)), _const_spec((1, D_MODEL))],
        out_specs=xblk,
        out_shape=jax.ShapeDtypeStruct(x4.shape, _F32),
        compiler_params=_params("parallel", "parallel"),
        name="s5_glu_layer",
    )(x4, yt, mod, d_skip.reshape(1, D_MODEL), w_glu.astype(_BF),
      ln_g.reshape(1, D_MODEL), ln_b.reshape(1, D_MODEL))
    return out.reshape(xt.shape)


def _trunk(x, mod, p, s5_tables):
    xt = x
    for i in range(DEPTH):
        j = i // 2
        if i % 2 == 0:
            xt = _conv_mixer_layer(xt, mod[i], p["sc_w_in"][j], p["sc_conv_w"][j],
                                   p["sc_conv_b"][j], p["sc_w_out"][j], p["ln1_g"][i], p["ln1_b"][i],
                                   seq_major_in=(i == 0))
        else:
            xt = _s5_mixer_layer(xt, mod[i], s5_tables[j], p["s5_d"][j], p["s5_w_glu"][j],
                                 p["ln1_g"][i], p["ln1_b"][i])
        xt = _ffn_layer(xt, mod[i], p["ffn_w_up"][i], p["ffn_conv_w"][i], p["ffn_conv_b"][i],
                        p["ffn_w_down"][i], p["ln2_g"][i], p["ln2_b"][i],
                        seq_major_out=(i == DEPTH - 1))
    return xt


def kernel(x_prompt, x_sample, c_prompt, c_sample, ada_w, ada_b, ln1_g, ln1_b, ln2_g, ln2_b, sc_w_in, sc_conv_w, sc_conv_b, sc_w_out, s5_a_re, s5_a_im, s5_log_dt, s5_b_re, s5_b_im, s5_c_re, s5_c_im, s5_d, s5_w_glu, ffn_w_up, ffn_conv_w, ffn_conv_b, ffn_w_down):
    p = dict(ada_w=ada_w, ada_b=ada_b, ln1_g=ln1_g, ln1_b=ln1_b, ln2_g=ln2_g, ln2_b=ln2_b,
             sc_w_in=sc_w_in, sc_conv_w=sc_conv_w, sc_conv_b=sc_conv_b, sc_w_out=sc_w_out,
             s5_d=s5_d, s5_w_glu=s5_w_glu, ffn_w_up=ffn_w_up, ffn_conv_w=ffn_conv_w,
             ffn_conv_b=ffn_conv_b, ffn_w_down=ffn_w_down)
    s5_tables = _s5_tables(s5_a_re, s5_a_im, s5_log_dt, s5_b_re, s5_b_im, s5_c_re, s5_c_im)
    n_prompt = c_prompt.shape[0]
    mod = _ada(jnp.concatenate([c_prompt, c_sample], axis=0), ada_w, ada_b)
    y_prompt = _trunk(x_prompt, mod[:, :, :n_prompt], p, s5_tables)
    y_sample = _trunk(x_sample, mod[:, :, n_prompt:], p, s5_tables)
    return (y_prompt, y_sample)
```

```python
import functools
import math

import jax
import jax.numpy as jnp
from jax import lax
from jax.experimental import pallas as pl
from jax.experimental.pallas import tpu as pltpu

D_MODEL = 1024
DEPTH = 4
FFN_HIDDEN = 2816
SSM_GROUP = 16
SSM_GROUPS = D_MODEL // SSM_GROUP
SSM_STATE = 64
DEEPNORM_ALPHA = (2 * DEPTH) ** 0.25
LN_EPS = 1e-5

SUBLANES = 8
LANES = 128
CHUNK = 16
CHUNK_COLS = CHUNK * SSM_GROUP
assert CHUNK & (CHUNK - 1) == 0
TOK_TILE = 128
GLU_TOKS = 16
GLU_SUB = 2
S5_GROUPS_PER_STEP = 2
TABLE_GROUPS_PER_STEP = 4
FFN_COL_CHUNKS = (1280, 1536)
ROW_SPLITS = 4
VMEM_LIMIT = 56 * 1024 * 1024

_HI = lax.Precision.HIGHEST
_BF = jnp.bfloat16
_F32 = jnp.float32


def _const_spec(shape):
    nd = len(shape)
    return pl.BlockSpec(shape, lambda *_: (0,) * nd, pipeline_mode=pl.Buffered(1))


def _params(*sem):
    return pltpu.CompilerParams(dimension_semantics=sem, vmem_limit_bytes=VMEM_LIMIT)


_GELU_C1 = math.sqrt(2.0 / math.pi)
_GELU_C2 = _GELU_C1 * 0.044715


def _gelu_half_gate(x):
    return 0.5 + 0.5 * jnp.tanh(x * (_GELU_C1 + _GELU_C2 * (x * x)))


def _gelu(x):
    return x * _gelu_half_gate(x)


def _gelu_times(x, v):
    return (x * v) * _gelu_half_gate(x)


def _layer_norm(h, g, b):
    mu = jnp.mean(h, axis=-1, keepdims=True)
    d = h - mu
    var = jnp.mean(d * d, axis=-1, keepdims=True)
    return d * lax.rsqrt(var + LN_EPS) * g + b


def _conv3_rows(cur, before, after, w, b):
    up = jnp.concatenate([before, cur[:-SUBLANES]], axis=0)
    dn = jnp.concatenate([cur[SUBLANES:], after], axis=0)
    return up * w[0:1, :] + cur * w[1:2, :] + dn * w[2:3, :] + b


def _ada_kernel(c_ref, w_ref, b_ref, o_ref):
    c = c_ref[...]
    ca = c * jax.nn.sigmoid(c)
    o_ref[...] = jnp.dot(ca, w_ref[...], precision=_HI, preferred_element_type=_F32) + b_ref[...]


def _ada(c, ada_w, ada_b):
    bsz = c.shape[0]
    return pl.pallas_call(
        _ada_kernel,
        grid=(DEPTH, 6),
        in_specs=[
            pl.BlockSpec((bsz, D_MODEL), lambda i, n: (0, 0)),
            pl.BlockSpec((None, D_MODEL, D_MODEL), lambda i, n: (i, 0, n)),
            pl.BlockSpec((None, 1, D_MODEL), lambda i, n: (i, 0, n)),
        ],
        out_specs=pl.BlockSpec((None, None, bsz, D_MODEL), lambda i, n: (i, n, 0, 0)),
        out_shape=jax.ShapeDtypeStruct((DEPTH, 6, bsz, D_MODEL), _F32),
        compiler_params=_params("arbitrary", "arbitrary"),
        name="ada_mod",
    )(c, ada_w, ada_b.reshape(DEPTH, 1, 6 * D_MODEL))


def _tile_specs(seq_len):
    main = pl.BlockSpec((TOK_TILE, SUBLANES, D_MODEL), lambda b, t: (t, b, 0))
    prev = pl.BlockSpec((1, SUBLANES, D_MODEL),
                        lambda b, t: (jnp.maximum(t * TOK_TILE - 1, 0), b, 0))
    nxt = pl.BlockSpec((1, SUBLANES, D_MODEL),
                       lambda b, t: (jnp.minimum((t + 1) * TOK_TILE, seq_len - 1), b, 0))
    return main, prev, nxt


def _mod_spec():
    return pl.BlockSpec((6, SUBLANES, D_MODEL), lambda b, t: (0, b, 0))


def _modulated_rows(xa, sc, sh):
    u = xa * sc[None] + sh[None]
    return u.reshape(xa.shape[0] * SUBLANES, D_MODEL).astype(_BF)


def _linear_step():
    return pl.program_id(0) * pl.num_programs(1) + pl.program_id(1)


def _gather_copies(x_hbm, buf, sem, slot, step, seq_len):
    tiles = seq_len // TOK_TILE
    seq0 = (step // tiles) * SUBLANES
    t0 = (step % tiles) * TOK_TILE
    spans = ((jnp.maximum(t0 - 1, 0), 1, 0), (t0, TOK_TILE, 1),
             (jnp.minimum(t0 + TOK_TILE, seq_len - 1), 1, TOK_TILE + 1))
    return [pltpu.make_async_copy(x_hbm.at[seq0 + b, pl.ds(src, n), :],
                                  buf.at[slot, pl.ds(dst, n), b, :], sem.at[slot])
            for b in range(SUBLANES) for src, n, dst in spans]


def _scatter_copies(buf, o_hbm, sem, slot, step, seq_len):
    tiles = seq_len // TOK_TILE
    seq0 = (step // tiles) * SUBLANES
    t0 = (step % tiles) * TOK_TILE
    return [pltpu.make_async_copy(buf.at[slot, :, b, :], o_hbm.at[seq0 + b, pl.ds(t0, TOK_TILE), :],
                                  sem.at[slot])
            for b in range(SUBLANES)]


def _gathered_tile(x_hbm, buf, sem, seq_len, n_steps):
    step = _linear_step()
    slot = step % 2

    @pl.when(step == 0)
    def _():
        for c in _gather_copies(x_hbm, buf, sem, slot, step, seq_len):
            c.start()

    @pl.when(step + 1 < n_steps)
    def _():
        for c in _gather_copies(x_hbm, buf, sem, 1 - slot, step + 1, seq_len):
            c.start()

    for c in _gather_copies(x_hbm, buf, sem, slot, step, seq_len):
        c.wait()
    return buf[slot]


def _scatter_slot(buf, o_hbm, sem, seq_len):
    step = _linear_step()
    slot = step % 2

    @pl.when(step >= 2)
    def _():
        for c in _scatter_copies(buf, o_hbm, sem, slot, step - 2, seq_len):
            c.wait()

    return buf.at[slot]


def _scatter_start(buf, o_hbm, sem, seq_len, n_steps):
    step = _linear_step()
    slot = step % 2
    for c in _scatter_copies(buf, o_hbm, sem, slot, step, seq_len):
        c.start()

    @pl.when(step == n_steps - 1)
    def _():
        if n_steps > 1:
            for c in _scatter_copies(buf, o_hbm, sem, 1 - slot, step - 1, seq_len):
                c.wait()
        for c in _scatter_copies(buf, o_hbm, sem, slot, step, seq_len):
            c.wait()


def _conv3_tile(full, first, last, w, b):
    n = full.shape[0]
    zero = jnp.zeros((SUBLANES, full.shape[1]), _F32)
    return _conv3_rows(full[SUBLANES:n - SUBLANES], jnp.where(first, zero, full[:SUBLANES]),
                       jnp.where(last, zero, full[n - SUBLANES:]), w, b)


def _conv_mixer_kernel(gather, *refs):
    if gather:
        x_hbm, mod_ref, win_ref, cw_ref, cb_ref, wout_ref, g_ref, b_ref, o_ref, buf, sem = refs
        xa = _gathered_tile(x_hbm, buf, sem, *gather)
    else:
        (x_ref, xp_ref, xn_ref, mod_ref, win_ref, cw_ref, cb_ref, wout_ref, g_ref, b_ref,
         o_ref) = refs
        xa = jnp.concatenate([xp_ref[...], x_ref[...], xn_ref[...]], axis=0)
    t = pl.program_id(1)
    first = t == 0
    last = t == pl.num_programs(1) - 1
    sh = mod_ref[0]
    sc = 1.0 + mod_ref[1]
    gate = 1.0 + mod_ref[2]
    x = xa[1:xa.shape[0] - 1]
    u = _modulated_rows(xa, sc, sh)
    p = jnp.dot(u, win_ref[...], preferred_element_type=_F32)
    q = p[:, D_MODEL:2 * D_MODEL] * p[:, 2 * D_MODEL:]
    cq = _conv3_tile(q, first, last, cw_ref[...], cb_ref[...])
    r = (p[SUBLANES:p.shape[0] - SUBLANES, :D_MODEL] * cq).astype(_BF)
    tok = x.shape[0] // ROW_SPLITS
    rows = tok * SUBLANES
    for i in range(ROW_SPLITS):
        xr = x[i * tok:(i + 1) * tok]
        m = jnp.dot(r[i * rows:(i + 1) * rows], wout_ref[...], preferred_element_type=_F32)
        h = DEEPNORM_ALPHA * xr + gate[None] * m.reshape(xr.shape)
        o_ref[i * tok:(i + 1) * tok] = _layer_norm(h, g_ref[...], b_ref[...])


def _conv_mixer_layer(x, mod, w_in, conv_w, conv_b, w_out, ln_g, ln_b, seq_major_in=False):
    if seq_major_in:
        bsz, seq_len, _ = x.shape
    else:
        seq_len, bsz, _ = x.shape
    grid = (bsz // SUBLANES, seq_len // TOK_TILE)
    main, prev, nxt = _tile_specs(seq_len)
    weights = [_const_spec((D_MODEL, 3 * D_MODEL)), _const_spec((3, D_MODEL)),
               _const_spec((1, D_MODEL)), _const_spec((D_MODEL, D_MODEL)),
               _const_spec((1, D_MODEL)), _const_spec((1, D_MODEL))]
    if seq_major_in:
        gather = (seq_len, grid[0] * grid[1])
        x_specs, x_args = [pl.BlockSpec(memory_space=pl.ANY)], (x,)
        scratch = [pltpu.VMEM((2, TOK_TILE + 2, SUBLANES, D_MODEL), _F32),
                   pltpu.SemaphoreType.DMA((2,))]
        sem = ("arbitrary", "arbitrary")
    else:
        gather = None
        x_specs, x_args = [main, prev, nxt], (x, x, x)
        scratch = []
        sem = ("parallel", "parallel")
    return pl.pallas_call(
        functools.partial(_conv_mixer_kernel, gather),
        grid=grid,
        in_specs=x_specs + [_mod_spec()] + weights,
        out_specs=main,
        out_shape=jax.ShapeDtypeStruct((seq_len, bsz, D_MODEL), _F32),
        scratch_shapes=scratch,
        compiler_params=_params(*sem),
        name="conv_mixer_layer",
    )(*x_args, mod, w_in.astype(_BF), conv_w, conv_b.reshape(1, D_MODEL), w_out.astype(_BF),
      ln_g.reshape(1, D_MODEL), ln_b.reshape(1, D_MODEL))


def _ffn_kernel(scatter, x_ref, xp_ref, xn_ref, mod_ref, wup_ref, cw_ref, cb_ref, wdn_ref, g_ref,
                b_ref, o_ref, *dma_scratch):
    dst_ref = _scatter_slot(dma_scratch[0], o_ref, dma_scratch[1], scatter[0]) if scatter else o_ref
    t = pl.program_id(1)
    first = t == 0
    last = t == pl.num_programs(1) - 1
    sh = mod_ref[3]
    sc = 1.0 + mod_ref[4]
    gate = 1.0 + mod_ref[5]
    x = x_ref[...]
    u = _modulated_rows(jnp.concatenate([xp_ref[...], x, xn_ref[...]], axis=0), sc, sh)
    tok = x.shape[0] // ROW_SPLITS
    rows = tok * SUBLANES
    accs = [None] * ROW_SPLITS
    k0 = 0
    for kc in FFN_COL_CHUNKS:
        halves = []
        for base in (0, FFN_HIDDEN):
            lo = base + k0
            a = jnp.dot(u, wup_ref[:, lo:lo + kc], preferred_element_type=_F32)
            halves.append(_conv3_tile(a, first, last, cw_ref[:, lo:lo + kc], cb_ref[:, lo:lo + kc]))
        hid = _gelu_times(halves[0], halves[1]).astype(_BF)
        for r in range(ROW_SPLITS):
            part = jnp.dot(hid[r * rows:(r + 1) * rows], wdn_ref[k0:k0 + kc, :],
                           preferred_element_type=_F32)
            accs[r] = part if accs[r] is None else accs[r] + part
        k0 += kc
    for r in range(ROW_SPLITS):
        xr = x[r * tok:(r + 1) * tok]
        h = DEEPNORM_ALPHA * xr + gate[None] * accs[r].reshape(xr.shape)
        dst_ref[r * tok:(r + 1) * tok] = _layer_norm(h, g_ref[...], b_ref[...])
    if scatter:
        _scatter_start(dma_scratch[0], o_ref, dma_scratch[1], *scatter)


def _ffn_layer(xt, mod, w_up, conv_w, conv_b, w_down, ln_g, ln_b, seq_major_out=False):
    seq_len, bsz, _ = xt.shape
    grid = (bsz // SUBLANES, seq_len // TOK_TILE)
    main, prev, nxt = _tile_specs(seq_len)
    if seq_major_out:
        scatter = (seq_len, grid[0] * grid[1])
        out_spec = pl.BlockSpec(memory_space=pl.ANY)
        out_shape = (bsz, seq_len, D_MODEL)
        scratch = [pltpu.VMEM((2, TOK_TILE, SUBLANES, D_MODEL), _F32), pltpu.SemaphoreType.DMA((2,))]
        sem = ("arbitrary", "arbitrary")
    else:
        scatter = None
        out_spec, out_shape, scratch = main, xt.shape, []
        sem = ("parallel", "parallel")
    return pl.pallas_call(
        functools.partial(_ffn_kernel, scatter),
        grid=grid,
        in_specs=[main, prev, nxt, _mod_spec(),
                  _const_spec((D_MODEL, 2 * FFN_HIDDEN)), _const_spec((3, 2 * FFN_HIDDEN)),
                  _const_spec((1, 2 * FFN_HIDDEN)), _const_spec((FFN_HIDDEN, D_MODEL)),
                  _const_spec((1, D_MODEL)), _const_spec((1, D_MODEL))],
        out_specs=out_spec,
        out_shape=jax.ShapeDtypeStruct(out_shape, _F32),
        scratch_shapes=scratch,
        compiler_params=_params(*sem),
        name="conv_ffn_layer",
    )(xt, xt, xt, mod, w_up.astype(_BF), conv_w, conv_b.reshape(1, 2 * FFN_HIDDEN),
      w_down.astype(_BF), ln_g.reshape(1, D_MODEL), ln_b.reshape(1, D_MODEL))


def _s5_tables(a_re, a_im, log_dt, b_re, b_im, c_re, c_im):
    n_lay, n_dir, n_grp, n_st = a_re.shape
    both = n_dir * n_st
    rows = n_lay * n_grp

    def states_last(v):
        return v.transpose(0, 2, 3, 1, 4).reshape(rows, v.shape[3], both)

    ldt = jnp.broadcast_to(log_dt[..., None], a_re.shape)
    par = states_last(jnp.stack([a_re, a_im, ldt], axis=3))
    b_col = jnp.stack([b_re, b_im], axis=2)
    b_col = b_col.transpose(0, 3, 2, 1, 4, 5).reshape(rows, 2, both, SSM_GROUP)
    b_row = jnp.stack([states_last(b_re.transpose(0, 1, 2, 4, 3)),
                       states_last(b_im.transpose(0, 1, 2, 4, 3))], axis=1)
    c_row = jnp.stack([states_last(c_re), states_last(c_im)], axis=1)
    blk = lambda *shape: pl.BlockSpec((TABLE_GROUPS_PER_STEP,) + shape,
                                      lambda g: (g,) + (0,) * len(shape))
    sq = (CHUNK_COLS, CHUNK_COLS)
    m_t, bt, ct_t, decay = pl.pallas_call(
        _s5_table_kernel,
        grid=(rows // TABLE_GROUPS_PER_STEP,),
        in_specs=[blk(3, both), blk(2, both, SSM_GROUP), blk(2, SSM_GROUP, both),
                  blk(2, SSM_GROUP, both)],
        out_specs=[blk(*sq), blk(*sq), blk(*sq), blk(2, both)],
        out_shape=[jax.ShapeDtypeStruct((rows,) + sq, _BF)] * 3
        + [jax.ShapeDtypeStruct((rows, 2, both), _F32)],
        compiler_params=_params("parallel"),
        name="s5_tables",
    )(par, b_col, b_row, c_row)
    per_layer = lambda v: v.reshape((n_lay, n_grp) + v.shape[1:])
    return [tuple(per_layer(v)[j] for v in (m_t, bt, ct_t, decay)) for j in range(n_lay)]


def _s5_table_kernel(*refs):
    for g in range(TABLE_GROUPS_PER_STEP):
        _s5_group_tables(*(r.at[g] for r in refs))


def _s5_group_tables(par_ref, bcol_ref, brow_ref, crow_ref, mt_ref, bt_ref, ct_ref, dec_ref):
    tt = CHUNK
    half = SSM_STATE
    both = 2 * half

    def discretise(a_re, a_im, log_dt):
        dt = jnp.exp(log_dt)
        mag = jnp.exp(a_re * dt)
        lb_re, lb_im = mag * jnp.cos(a_im * dt), mag * jnp.sin(a_im * dt)
        den = a_re * a_re + a_im * a_im
        f_re = ((lb_re - 1.0) * a_re + lb_im * a_im) / den
        f_im = (lb_im * a_re - (lb_re - 1.0) * a_im) / den
        squares = [(lb_re, lb_im)]
        while len(squares) <= tt.bit_length() - 1:
            r, i = squares[-1]
            squares.append((r * r - i * i, 2.0 * (r * i)))

        def power(n):
            p_re = jnp.ones(n.shape, _F32)
            p_im = jnp.zeros(n.shape, _F32)
            for k, (s_re, s_im) in enumerate(squares):
                bit = ((n >> k) & 1) == 1
                p_re, p_im = (jnp.where(bit, p_re * s_re - p_im * s_im, p_re),
                              jnp.where(bit, p_re * s_im + p_im * s_re, p_im))
            return p_re, p_im

        return power, squares[-1], f_re, f_im

    power, lb_chunk, f_re, f_im = discretise(par_ref[0:1, :], par_ref[1:2, :], par_ref[2:3, :])
    tok = lax.broadcasted_iota(jnp.int32, (tt, both), 0)
    fwd = lax.broadcasted_iota(jnp.int32, (tt, both), 1) < half
    p_re, p_im = power(jnp.where(fwd, tt - 1 - tok, tok))
    w_re = (p_re * f_re - p_im * f_im)[:, None, :]
    w_im = (p_re * f_im + p_im * f_re)[:, None, :]
    b_re, b_im = brow_ref[0][None], brow_ref[1][None]
    bt_ref[:, 0:both] = (w_re * b_re - w_im * b_im).reshape(CHUNK_COLS, both).astype(_BF)
    bt_ref[:, both:] = (w_re * b_im + w_im * b_re).reshape(CHUNK_COLS, both).astype(_BF)
    e_re, e_im = power(jnp.where(fwd, tok + 1, tt - tok))
    e_re, e_im = e_re[:, None, :], e_im[:, None, :]
    c_re, c_im = crow_ref[0], crow_ref[1]
    ct_ref[:, 0:both] = (c_re[None] * e_re - c_im[None] * e_im).reshape(CHUNK_COLS, both).astype(_BF)
    ct_ref[:, both:] = (-(c_re[None] * e_im + c_im[None] * e_re)).reshape(CHUNK_COLS, both).astype(_BF)
    dec_ref[0:1, :] = lb_chunk[0]
    dec_ref[1:2, :] = lb_chunk[1]

    n_s = 2 * tt
    s_idx = lax.broadcasted_iota(jnp.int32, (n_s, both), 0)
    fwd = lax.broadcasted_iota(jnp.int32, (n_s, both), 1) < half
    expo = jnp.where(fwd, tt - 1 - s_idx, s_idx - (tt - 1))
    live = expo >= 0
    p_re, p_im = power(jnp.maximum(expo, 0))
    w_re = jnp.where(live, p_re * f_re - p_im * f_im, 0.0).T
    w_im = jnp.where(live, p_re * f_im + p_im * f_re, 0.0).T
    width = n_s * SSM_GROUP

    def spread(v, pick):
        k = v.shape[1]
        row = lax.broadcasted_iota(jnp.int32, (3 * k, width), 0)
        onehot = (pick(lax.broadcasted_iota(jnp.int32, (3 * k, width), 1)) == row % k).astype(_BF)
        hi = v.astype(_BF)
        rest = v - hi.astype(_F32)
        mid = rest.astype(_BF)
        lo = (rest - mid.astype(_F32)).astype(_BF)
        return jnp.dot(jnp.concatenate([hi, mid, lo], axis=1), onehot, preferred_element_type=_F32)

    def dot3(a, b):
        a_hi, b_hi = a.astype(_BF), b.astype(_BF)
        a_lo, b_lo = (a - a_hi.astype(_F32)).astype(_BF), (b - b_hi.astype(_F32)).astype(_BF)
        d = functools.partial(jnp.dot, preferred_element_type=_F32)
        return d(a_hi, b_hi) + (d(a_hi, b_lo) + d(a_lo, b_hi))

    w_re, w_im = (spread(w, lambda l: l // SSM_GROUP) for w in (w_re, w_im))
    b_re, b_im = (spread(bcol_ref[k], lambda l: l % SSM_GROUP) for k in (0, 1))
    strip = dot3(c_re, w_re * b_re - w_im * b_im) - dot3(c_im, w_re * b_im + w_im * b_re)
    for t in range(tt):
        lo = SSM_GROUP * (tt - 1 - t)
        mt_ref[SSM_GROUP * t:SSM_GROUP * (t + 1), :] = strip[:, lo:lo + CHUNK_COLS].astype(_BF)


def _s5_regroup_kernel(x_ref, mod_ref, o_ref):
    sh = mod_ref[0]
    sc = 1.0 + mod_ref[1]
    cols = x_ref.shape[0] * x_ref.shape[2]
    for tau in range(CHUNK):
        slab = (x_ref[:, tau] * sc[None] + sh[None]).reshape(cols, D_MODEL)
        o_ref[:, SSM_GROUP * tau:SSM_GROUP * (tau + 1), :] = (
            slab.T.astype(_BF).reshape(SSM_GROUPS, SSM_GROUP, cols))


def _s5_chunk_kernel(bsz, n_chunks, z_ref, mt_ref, bt_ref, ct_ref, dec_ref, y_ref,
                     loc_ref, carry_ref, intra_ref):
    half = SSM_STATE
    n_grp = z_ref.shape[0]
    for g in range(n_grp):
        loc_ref[g] = lax.dot_general(z_ref[g], bt_ref[g], (((0,), (0,)), ((), ())),
                                     preferred_element_type=_F32)
    a_re = [jnp.broadcast_to(dec_ref[g, 0:1, :], (bsz, 2 * half)) for g in range(n_grp)]
    a_im = [jnp.broadcast_to(dec_ref[g, 1:2, :], (bsz, 2 * half)) for g in range(n_grp)]
    is_fwd = lax.broadcasted_iota(jnp.int32, (bsz, 2 * half), 1) < half

    def step(k, state):
        rf = k * bsz
        rb = (n_chunks - 1 - k) * bsz
        new = []
        for g in range(n_grp):
            s_re, s_im = state[2 * g], state[2 * g + 1]
            carry_ref[g, pl.ds(rf, bsz), 0:half] = s_re[:, 0:half]
            carry_ref[g, pl.ds(rb, bsz), half:2 * half] = s_re[:, half:]
            carry_ref[g, pl.ds(rf, bsz), 2 * half:3 * half] = s_im[:, 0:half]
            carry_ref[g, pl.ds(rb, bsz), 3 * half:] = s_im[:, half:]
            l_re = jnp.where(is_fwd, loc_ref[g, pl.ds(rf, bsz), 0:2 * half],
                             loc_ref[g, pl.ds(rb, bsz), 0:2 * half])
            l_im = jnp.where(is_fwd, loc_ref[g, pl.ds(rf, bsz), 2 * half:],
                             loc_ref[g, pl.ds(rb, bsz), 2 * half:])
            new.append(a_re[g] * s_re - a_im[g] * s_im + l_re)
            new.append(a_re[g] * s_im + a_im[g] * s_re + l_im)
        return tuple(new)

    for g in range(n_grp):
        intra_ref[g] = jnp.dot(mt_ref[g], z_ref[g], preferred_element_type=_F32)
    state = (jnp.zeros((bsz, 2 * half), _F32),) * (2 * n_grp)
    for k in range(n_chunks):
        state = step(k, state)
    for g in range(n_grp):
        y = intra_ref[g] + lax.dot_general(ct_ref[g], carry_ref[g].astype(_BF),
                                           (((1,), (1,)), ((), ())), preferred_element_type=_F32)
        y_ref[g] = y.astype(_BF)


def _s5_glu_kernel(x_ref, yt_ref, mod_ref, d_ref, wglu_ref, g_ref, b_ref, o_ref):
    sh = mod_ref[0]
    sc = 1.0 + mod_ref[1]
    gate = 1.0 + mod_ref[2]
    ncb, ntok, bsz, _ = x_ref.shape
    cols = ncb * bsz
    for t0 in range(0, ntok, GLU_SUB):
        zs = []
        for tq in range(t0, t0 + GLU_SUB):
            yd = yt_ref[:, SSM_GROUP * tq:SSM_GROUP * (tq + 1), :].astype(_F32)
            yd = yd.reshape(D_MODEL, cols).T.reshape(ncb, bsz, D_MODEL)
            u = x_ref[:, tq] * sc[None] + sh[None]
            zs.append(_gelu(d_ref[...] * u + yd).reshape(cols, D_MODEL).astype(_BF))
        pz = jnp.dot(jnp.concatenate(zs, axis=0), wglu_ref[...], preferred_element_type=_F32)
        m = pz[:, :D_MODEL] * jax.nn.sigmoid(pz[:, D_MODEL:])
        for k in range(GLU_SUB):
            mt = m[cols * k:cols * (k + 1)].reshape(ncb, bsz, D_MODEL)
            h = DEEPNORM_ALPHA * x_ref[:, t0 + k] + gate[None] * mt
            o_ref[:, t0 + k] = _layer_norm(h, g_ref[...], b_ref[...])


def _s5_mixer_layer(xt, mod, tables, d_skip, w_glu, ln_g, ln_b):
    seq_len, bsz, _ = xt.shape
    n_chunks = seq_len // CHUNK
    ncb = LANES // bsz
    cols = n_chunks * bsz
    x4 = xt.reshape(n_chunks, CHUNK, bsz, D_MODEL)
    mod_full = pl.BlockSpec((6, bsz, D_MODEL), lambda *_: (0, 0, 0))
    zt = pl.pallas_call(
        _s5_regroup_kernel,
        grid=(n_chunks // ncb,),
        in_specs=[pl.BlockSpec((ncb, CHUNK, bsz, D_MODEL), lambda i: (i, 0, 0, 0)), mod_full],
        out_specs=pl.BlockSpec((SSM_GROUPS, CHUNK_COLS, LANES), lambda i: (0, 0, i)),
        out_shape=jax.ShapeDtypeStruct((SSM_GROUPS, CHUNK_COLS, cols), _BF),
        compiler_params=_params("parallel"),
        name="s5_regroup",
    )(x4, mod)
    gs = S5_GROUPS_PER_STEP
    grp = lambda r, c: pl.BlockSpec((gs, r, c), lambda g: (g, 0, 0))
    m_t, bt, ct_t, decay = tables
    yt = pl.pallas_call(
        functools.partial(_s5_chunk_kernel, bsz, n_chunks),
        grid=(SSM_GROUPS // gs,),
        in_specs=[grp(CHUNK_COLS, cols), grp(CHUNK_COLS, CHUNK_COLS), grp(CHUNK_COLS, CHUNK_COLS),
                  grp(CHUNK_COLS, CHUNK_COLS), grp(2, 2 * SSM_STATE)],
        out_specs=grp(CHUNK_COLS, cols),
        out_shape=jax.ShapeDtypeStruct((SSM_GROUPS, CHUNK_COLS, cols), _BF),
        scratch_shapes=[pltpu.VMEM((gs, cols, CHUNK_COLS), _F32),
                        pltpu.VMEM((gs, cols, CHUNK_COLS), _F32),
                        pltpu.VMEM((gs, CHUNK_COLS, cols), _F32)],
        compiler_params=_params("parallel"),
        name="s5_chunk_scan",
    )(zt, m_t, bt, ct_t, decay)
    xblk = pl.BlockSpec((ncb, GLU_TOKS, bsz, D_MODEL), lambda i, q: (i, q, 0, 0))
    out = pl.pallas_call(
        _s5_glu_kernel,
        grid=(n_chunks // ncb, CHUNK // GLU_TOKS),
        in_specs=[xblk,
                  pl.BlockSpec((SSM_GROUPS, SSM_GROUP * GLU_TOKS, LANES), lambda i, q: (0, q, i)),
                  mod_full, _const_spec((1, D_MODEL)), _const_spec((D_MODEL, 2 * D_MODEL)),
                  _const_spec((1, D_MODEL)), _const_spec((1, D_MODEL))],
        out_specs=xblk,
        out_shape=jax.ShapeDtypeStruct(x4.shape, _F32),
        compiler_params=_params("parallel", "parallel"),
        name="s5_glu_layer",
    )(x4, yt, mod, d_skip.reshape(1, D_MODEL), w_glu.astype(_BF),
      ln_g.reshape(1, D_MODEL), ln_b.reshape(1, D_MODEL))
    return out.reshape(xt.shape)


def _trunk(x, mod, p, s5_tables):
    xt = x
    for i in range(DEPTH):
        j = i // 2
        if i % 2 == 0:
            xt = _conv_mixer_layer(xt, mod[i], p["sc_w_in"][j], p["sc_conv_w"][j],
                                   p["sc_conv_b"][j], p["sc_w_out"][j], p["ln1_g"][i], p["ln1_b"][i],
                                   seq_major_in=(i == 0))
        else:
            xt = _s5_mixer_layer(xt, mod[i], s5_tables[j], p["s5_d"][j], p["s5_w_glu"][j],
                                 p["ln1_g"][i], p["ln1_b"][i])
        xt = _ffn_layer(xt, mod[i], p["ffn_w_up"][i], p["ffn_conv_w"][i], p["ffn_conv_b"][i],
                        p["ffn_w_down"][i], p["ln2_g"][i], p["ln2_b"][i],
                        seq_major_out=(i == DEPTH - 1))
    return xt


def kernel(x_prompt, x_sample, c_prompt, c_sample, ada_w, ada_b, ln1_g, ln1_b, ln2_g, ln2_b, sc_w_in, sc_conv_w, sc_conv_b, sc_w_out, s5_a_re, s5_a_im, s5_log_dt, s5_b_re, s5_b_im, s5_c_re, s5_c_im, s5_d, s5_w_glu, ffn_w_up, ffn_conv_w, ffn_conv_b, ffn_w_down):
    p = dict(ada_w=ada_w, ada_b=ada_b, ln1_g=ln1_g, ln1_b=ln1_b, ln2_g=ln2_g, ln2_b=ln2_b,
             sc_w_in=sc_w_in, sc_conv_w=sc_conv_w, sc_conv_b=sc_conv_b, sc_w_out=sc_w_out,
             s5_d=s5_d, s5_w_glu=s5_w_glu, ffn_w_up=ffn_w_up, ffn_conv_w=ffn_conv_w,
             ffn_conv_b=ffn_conv_b, ffn_w_down=ffn_w_down)
    s5_tables = _s5_tables(s5_a_re, s5_a_im, s5_log_dt, s5_b_re, s5_b_im, s5_c_re, s5_c_im)
    n_prompt = c_prompt.shape[0]
    mod = _ada(jnp.concatenate([c_prompt, c_sample], axis=0), ada_w, ada_b)
    y_prompt = _trunk(x_prompt, mod[:, :, :n_prompt], p, s5_tables)
    y_sample = _trunk(x_sample, mod[:, :, n_prompt:], p, s5_tables)
    return (y_prompt, y_sample)
```

```python
import functools
import math

import jax
import jax.numpy as jnp
from jax import lax
from jax.experimental import pallas as pl
from jax.experimental.pallas import tpu as pltpu

D_MODEL = 1024
DEPTH = 4
FFN_HIDDEN = 2816
SSM_GROUP = 16
SSM_GROUPS = D_MODEL // SSM_GROUP
SSM_STATE = 64
DEEPNORM_ALPHA = (2 * DEPTH) ** 0.25
LN_EPS = 1e-5

SUBLANES = 8
LANES = 128
CHUNK = 16
CHUNK_COLS = CHUNK * SSM_GROUP
assert CHUNK & (CHUNK - 1) == 0
TOK_TILE = 128
GLU_TOKS = 16
GLU_SUB = 2
S5_GROUPS_PER_STEP = 2
TABLE_GROUPS_PER_STEP = 4
FFN_COL_CHUNKS = (1280, 1536)
ROW_SPLITS = 4
VMEM_LIMIT = 56 * 1024 * 1024

_HI = lax.Precision.HIGHEST
_BF = jnp.bfloat16
_F32 = jnp.float32


def _layer_spec(idx, shape):
    nd = len(shape)
    return pl.BlockSpec((None,) + shape, lambda *_: (idx,) + (0,) * nd, pipeline_mode=pl.Buffered(1))


def _params(*sem):
    return pltpu.CompilerParams(dimension_semantics=sem, vmem_limit_bytes=VMEM_LIMIT)


_GELU_C1 = math.sqrt(2.0 / math.pi)
_GELU_C2 = _GELU_C1 * 0.044715


def _gelu_half_gate(x):
    return 0.5 + 0.5 * jnp.tanh(x * (_GELU_C1 + _GELU_C2 * (x * x)))


def _gelu(x):
    return x * _gelu_half_gate(x)


def _gelu_times(x, v):
    return (x * v) * _gelu_half_gate(x)


def _layer_norm(h, g, b):
    mu = jnp.mean(h, axis=-1, keepdims=True)
    d = h - mu
    var = jnp.mean(d * d, axis=-1, keepdims=True)
    return d * lax.rsqrt(var + LN_EPS) * g + b


def _conv3_rows(cur, before, after, w, b):
    up = jnp.concatenate([before, cur[:-SUBLANES]], axis=0)
    dn = jnp.concatenate([cur[SUBLANES:], after], axis=0)
    return up * w[0:1, :] + cur * w[1:2, :] + dn * w[2:3, :] + b


def _ada_kernel(c_ref, w_ref, b_ref, o_ref):
    c = c_ref[...]
    ca = c * jax.nn.sigmoid(c)
    o_ref[...] = jnp.dot(ca, w_ref[...], precision=_HI, preferred_element_type=_F32) + b_ref[...]


def _ada(c, ada_w, ada_b):
    bsz = c.shape[0]
    return pl.pallas_call(
        _ada_kernel,
        grid=(DEPTH, 6),
        in_specs=[
            pl.BlockSpec((bsz, D_MODEL), lambda i, n: (0, 0)),
            pl.BlockSpec((None, D_MODEL, D_MODEL), lambda i, n: (i, 0, n)),
            pl.BlockSpec((None, 1, D_MODEL), lambda i, n: (i, 0, n)),
        ],
        out_specs=pl.BlockSpec((None, None, bsz, D_MODEL), lambda i, n: (i, n, 0, 0)),
        out_shape=jax.ShapeDtypeStruct((DEPTH, 6, bsz, D_MODEL), _F32),
        compiler_params=_params("arbitrary", "arbitrary"),
        name="ada_mod",
    )(c, ada_w, ada_b.reshape(DEPTH, 1, 6 * D_MODEL))


def _tile_specs(seq_len):
    main = pl.BlockSpec((TOK_TILE, SUBLANES, D_MODEL), lambda b, t: (t, b, 0))
    prev = pl.BlockSpec((1, SUBLANES, D_MODEL),
                        lambda b, t: (jnp.maximum(t * TOK_TILE - 1, 0), b, 0))
    nxt = pl.BlockSpec((1, SUBLANES, D_MODEL),
                       lambda b, t: (jnp.minimum((t + 1) * TOK_TILE, seq_len - 1), b, 0))
    return main, prev, nxt


def _mod_spec():
    return pl.BlockSpec((6, SUBLANES, D_MODEL), lambda b, t: (0, b, 0))


def _modulated_rows(xa, sc, sh):
    u = xa * sc[None] + sh[None]
    return u.reshape(xa.shape[0] * SUBLANES, D_MODEL).astype(_BF)


def _linear_step():
    return pl.program_id(0) * pl.num_programs(1) + pl.program_id(1)


def _gather_copies(x_hbm, buf, sem, slot, step, seq_len):
    tiles = seq_len // TOK_TILE
    seq0 = (step // tiles) * SUBLANES
    t0 = (step % tiles) * TOK_TILE
    spans = ((jnp.maximum(t0 - 1, 0), 1, 0), (t0, TOK_TILE, 1),
             (jnp.minimum(t0 + TOK_TILE, seq_len - 1), 1, TOK_TILE + 1))
    return [pltpu.make_async_copy(x_hbm.at[seq0 + b, pl.ds(src, n), :],
                                  buf.at[slot, pl.ds(dst, n), b, :], sem.at[slot])
            for b in range(SUBLANES) for src, n, dst in spans]


def _scatter_copies(buf, o_hbm, sem, slot, step, seq_len):
    tiles = seq_len // TOK_TILE
    seq0 = (step // tiles) * SUBLANES
    t0 = (step % tiles) * TOK_TILE
    return [pltpu.make_async_copy(buf.at[slot, :, b, :], o_hbm.at[seq0 + b, pl.ds(t0, TOK_TILE), :],
                                  sem.at[slot])
            for b in range(SUBLANES)]


def _gathered_tile(x_hbm, buf, sem, seq_len, n_steps):
    step = _linear_step()
    slot = step % 2

    @pl.when(step == 0)
    def _():
        for c in _gather_copies(x_hbm, buf, sem, slot, step, seq_len):
            c.start()

    @pl.when(step + 1 < n_steps)
    def _():
        for c in _gather_copies(x_hbm, buf, sem, 1 - slot, step + 1, seq_len):
            c.start()

    for c in _gather_copies(x_hbm, buf, sem, slot, step, seq_len):
        c.wait()
    return buf[slot]


def _scatter_slot(buf, o_hbm, sem, seq_len):
    step = _linear_step()
    slot = step % 2

    @pl.when(step >= 2)
    def _():
        for c in _scatter_copies(buf, o_hbm, sem, slot, step - 2, seq_len):
            c.wait()

    return buf.at[slot]


def _scatter_start(buf, o_hbm, sem, seq_len, n_steps):
    step = _linear_step()
    slot = step % 2
    for c in _scatter_copies(buf, o_hbm, sem, slot, step, seq_len):
        c.start()

    @pl.when(step == n_steps - 1)
    def _():
        if n_steps > 1:
            for c in _scatter_copies(buf, o_hbm, sem, 1 - slot, step - 1, seq_len):
                c.wait()
        for c in _scatter_copies(buf, o_hbm, sem, slot, step, seq_len):
            c.wait()


def _conv3_tile(full, first, last, w, b):
    n = full.shape[0]
    zero = jnp.zeros((SUBLANES, full.shape[1]), _F32)
    return _conv3_rows(full[SUBLANES:n - SUBLANES], jnp.where(first, zero, full[:SUBLANES]),
                       jnp.where(last, zero, full[n - SUBLANES:]), w, b)


def _conv_mixer_kernel(gather, *refs):
    if gather:
        x_hbm, mod_ref, win_ref, cw_ref, cb_ref, wout_ref, g_ref, b_ref, o_ref, buf, sem = refs
        xa = _gathered_tile(x_hbm, buf, sem, *gather)
    else:
        (x_ref, xp_ref, xn_ref, mod_ref, win_ref, cw_ref, cb_ref, wout_ref, g_ref, b_ref,
         o_ref) = refs
        xa = jnp.concatenate([xp_ref[...], x_ref[...], xn_ref[...]], axis=0)
    t = pl.program_id(1)
    first = t == 0
    last = t == pl.num_programs(1) - 1
    sh = mod_ref[0]
    sc = 1.0 + mod_ref[1]
    gate = 1.0 + mod_ref[2]
    x = xa[1:xa.shape[0] - 1]
    u = _modulated_rows(xa, sc, sh)
    p = jnp.dot(u, win_ref[...], preferred_element_type=_F32)
    q = p[:, D_MODEL:2 * D_MODEL] * p[:, 2 * D_MODEL:]
    cq = _conv3_tile(q, first, last, cw_ref[...], cb_ref[...])
    r = (p[SUBLANES:p.shape[0] - SUBLANES, :D_MODEL] * cq).astype(_BF)
    tok = x.shape[0] // ROW_SPLITS
    rows = tok * SUBLANES
    for i in range(ROW_SPLITS):
        xr = x[i * tok:(i + 1) * tok]
        m = jnp.dot(r[i * rows:(i + 1) * rows], wout_ref[...], preferred_element_type=_F32)
        h = DEEPNORM_ALPHA * xr + gate[None] * m.reshape(xr.shape)
        o_ref[i * tok:(i + 1) * tok] = _layer_norm(h, g_ref[...], b_ref[...])


def _conv_mixer_layer(x, mod, p, i, seq_major_in=False):
    j = i // 2
    if seq_major_in:
        bsz, seq_len, _ = x.shape
    else:
        seq_len, bsz, _ = x.shape
    grid = (bsz // SUBLANES, seq_len // TOK_TILE)
    main, prev, nxt = _tile_specs(seq_len)
    weights = [_layer_spec(j, (D_MODEL, 3 * D_MODEL)), _layer_spec(j, (3, D_MODEL)),
               _layer_spec(j, (1, D_MODEL)), _layer_spec(j, (D_MODEL, D_MODEL)),
               _layer_spec(i, (1, D_MODEL)), _layer_spec(i, (1, D_MODEL))]
    if seq_major_in:
        gather = (seq_len, grid[0] * grid[1])
        x_specs, x_args = [pl.BlockSpec(memory_space=pl.ANY)], (x,)
        scratch = [pltpu.VMEM((2, TOK_TILE + 2, SUBLANES, D_MODEL), _F32),
                   pltpu.SemaphoreType.DMA((2,))]
        sem = ("arbitrary", "arbitrary")
    else:
        gather = None
        x_specs, x_args = [main, prev, nxt], (x, x, x)
        scratch = []
        sem = ("parallel", "parallel")
    return pl.pallas_call(
        functools.partial(_conv_mixer_kernel, gather),
        grid=grid,
        in_specs=x_specs + [_mod_spec()] + weights,
        out_specs=main,
        out_shape=jax.ShapeDtypeStruct((seq_len, bsz, D_MODEL), _F32),
        scratch_shapes=scratch,
        compiler_params=_params(*sem),
        name="conv_mixer_layer",
    )(*x_args, mod, p["sc_w_in"], p["sc_conv_w"], p["sc_conv_b"], p["sc_w_out"],
      p["ln1_g"], p["ln1_b"])


def _ffn_kernel(scatter, x_ref, xp_ref, xn_ref, mod_ref, wup_ref, cw_ref, cb_ref, wdn_ref, g_ref,
                b_ref, o_ref, *dma_scratch):
    dst_ref = _scatter_slot(dma_scratch[0], o_ref, dma_scratch[1], scatter[0]) if scatter else o_ref
    t = pl.program_id(1)
    first = t == 0
    last = t == pl.num_programs(1) - 1
    sh = mod_ref[3]
    sc = 1.0 + mod_ref[4]
    gate = 1.0 + mod_ref[5]
    x = x_ref[...]
    u = _modulated_rows(jnp.concatenate([xp_ref[...], x, xn_ref[...]], axis=0), sc, sh)
    tok = x.shape[0] // ROW_SPLITS
    rows = tok * SUBLANES
    accs = [None] * ROW_SPLITS
    k0 = 0
    for kc in FFN_COL_CHUNKS:
        halves = []
        for base in (0, FFN_HIDDEN):
            lo = base + k0
            a = jnp.dot(u, wup_ref[:, lo:lo + kc], preferred_element_type=_F32)
            halves.append(_conv3_tile(a, first, last, cw_ref[:, lo:lo + kc], cb_ref[:, lo:lo + kc]))
        hid = _gelu_times(halves[0], halves[1]).astype(_BF)
        for r in range(ROW_SPLITS):
            part = jnp.dot(hid[r * rows:(r + 1) * rows], wdn_ref[k0:k0 + kc, :],
                           preferred_element_type=_F32)
            accs[r] = part if accs[r] is None else accs[r] + part
        k0 += kc
    for r in range(ROW_SPLITS):
        xr = x[r * tok:(r + 1) * tok]
        h = DEEPNORM_ALPHA * xr + gate[None] * accs[r].reshape(xr.shape)
        dst_ref[r * tok:(r + 1) * tok] = _layer_norm(h, g_ref[...], b_ref[...])
    if scatter:
        _scatter_start(dma_scratch[0], o_ref, dma_scratch[1], *scatter)


def _ffn_layer(xt, mod, p, i, seq_major_out=False):
    seq_len, bsz, _ = xt.shape
    grid = (bsz // SUBLANES, seq_len // TOK_TILE)
    main, prev, nxt = _tile_specs(seq_len)
    if seq_major_out:
        scatter = (seq_len, grid[0] * grid[1])
        out_spec = pl.BlockSpec(memory_space=pl.ANY)
        out_shape = (bsz, seq_len, D_MODEL)
        scratch = [pltpu.VMEM((2, TOK_TILE, SUBLANES, D_MODEL), _F32), pltpu.SemaphoreType.DMA((2,))]
        sem = ("arbitrary", "arbitrary")
    else:
        scatter = None
        out_spec, out_shape, scratch = main, xt.shape, []
        sem = ("parallel", "parallel")
    return pl.pallas_call(
        functools.partial(_ffn_kernel, scatter),
        grid=grid,
        in_specs=[main, prev, nxt, _mod_spec(),
                  _layer_spec(i, (D_MODEL, 2 * FFN_HIDDEN)), _layer_spec(i, (3, 2 * FFN_HIDDEN)),
                  _layer_spec(i, (1, 2 * FFN_HIDDEN)), _layer_spec(i, (FFN_HIDDEN, D_MODEL)),
                  _layer_spec(i, (1, D_MODEL)), _layer_spec(i, (1, D_MODEL))],
        out_specs=out_spec,
        out_shape=jax.ShapeDtypeStruct(out_shape, _F32),
        scratch_shapes=scratch,
        compiler_params=_params(*sem),
        name="conv_ffn_layer",
    )(xt, xt, xt, mod, p["ffn_w_up"], p["ffn_conv_w"], p["ffn_conv_b"], p["ffn_w_down"],
      p["ln2_g"], p["ln2_b"])


def _s5_tables(a_re, a_im, log_dt, b_re, b_im, c_re, c_im):
    n_lay, n_dir, n_grp, n_st = a_re.shape
    both = n_dir * n_st
    rows = n_lay * n_grp

    def states_last(v):
        return v.transpose(0, 2, 3, 1, 4).reshape(rows, v.shape[3], both)

    ldt = jnp.broadcast_to(log_dt[..., None], a_re.shape)
    par = states_last(jnp.stack([a_re, a_im, ldt], axis=3))
    b_col = jnp.stack([b_re, b_im], axis=2)
    b_col = b_col.transpose(0, 3, 2, 1, 4, 5).reshape(rows, 2, both, SSM_GROUP)
    b_row = jnp.stack([states_last(b_re.transpose(0, 1, 2, 4, 3)),
                       states_last(b_im.transpose(0, 1, 2, 4, 3))], axis=1)
    c_row = jnp.stack([states_last(c_re), states_last(c_im)], axis=1)
    blk = lambda *shape: pl.BlockSpec((TABLE_GROUPS_PER_STEP,) + shape,
                                      lambda g: (g,) + (0,) * len(shape))
    sq = (CHUNK_COLS, CHUNK_COLS)
    return pl.pallas_call(
        _s5_table_kernel,
        grid=(rows // TABLE_GROUPS_PER_STEP,),
        in_specs=[blk(3, both), blk(2, both, SSM_GROUP), blk(2, SSM_GROUP, both),
                  blk(2, SSM_GROUP, both)],
        out_specs=[blk(*sq), blk(*sq), blk(*sq), blk(2, both)],
        out_shape=[jax.ShapeDtypeStruct((rows,) + sq, _BF)] * 3
        + [jax.ShapeDtypeStruct((rows, 2, both), _F32)],
        compiler_params=_params("parallel"),
        name="s5_tables",
    )(par, b_col, b_row, c_row)


def _s5_table_kernel(*refs):
    for g in range(TABLE_GROUPS_PER_STEP):
        _s5_group_tables(*(r.at[g] for r in refs))


def _s5_group_tables(par_ref, bcol_ref, brow_ref, crow_ref, mt_ref, bt_ref, ct_ref, dec_ref):
    tt = CHUNK
    half = SSM_STATE
    both = 2 * half

    def discretise(a_re, a_im, log_dt):
        dt = jnp.exp(log_dt)
        mag = jnp.exp(a_re * dt)
        lb_re, lb_im = mag * jnp.cos(a_im * dt), mag * jnp.sin(a_im * dt)
        den = a_re * a_re + a_im * a_im
        f_re = ((lb_re - 1.0) * a_re + lb_im * a_im) / den
        f_im = (lb_im * a_re - (lb_re - 1.0) * a_im) / den
        squares = [(lb_re, lb_im)]
        while len(squares) <= tt.bit_length() - 1:
            r, i = squares[-1]
            squares.append((r * r - i * i, 2.0 * (r * i)))

        def power(n):
            p_re = jnp.ones(n.shape, _F32)
            p_im = jnp.zeros(n.shape, _F32)
            for k, (s_re, s_im) in enumerate(squares):
                bit = ((n >> k) & 1) == 1
                p_re, p_im = (jnp.where(bit, p_re * s_re - p_im * s_im, p_re),
                              jnp.where(bit, p_re * s_im + p_im * s_re, p_im))
            return p_re, p_im

        return power, squares[-1], f_re, f_im

    power, lb_chunk, f_re, f_im = discretise(par_ref[0:1, :], par_ref[1:2, :], par_ref[2:3, :])
    tok = lax.broadcasted_iota(jnp.int32, (tt, both), 0)
    fwd = lax.broadcasted_iota(jnp.int32, (tt, both), 1) < half
    p_re, p_im = power(jnp.where(fwd, tt - 1 - tok, tok))
    w_re = (p_re * f_re - p_im * f_im)[:, None, :]
    w_im = (p_re * f_im + p_im * f_re)[:, None, :]
    b_re, b_im = brow_ref[0][None], brow_ref[1][None]
    bt_ref[:, 0:both] = (w_re * b_re - w_im * b_im).reshape(CHUNK_COLS, both).astype(_BF)
    bt_ref[:, both:] = (w_re * b_im + w_im * b_re).reshape(CHUNK_COLS, both).astype(_BF)
    e_re, e_im = power(jnp.where(fwd, tok + 1, tt - tok))
    e_re, e_im = e_re[:, None, :], e_im[:, None, :]
    c_re, c_im = crow_ref[0], crow_ref[1]
    ct_ref[:, 0:both] = (c_re[None] * e_re - c_im[None] * e_im).reshape(CHUNK_COLS, both).astype(_BF)
    ct_ref[:, both:] = (-(c_re[None] * e_im + c_im[None] * e_re)).reshape(CHUNK_COLS, both).astype(_BF)
    dec_ref[0:1, :] = lb_chunk[0]
    dec_ref[1:2, :] = lb_chunk[1]

    n_s = 2 * tt
    s_idx = lax.broadcasted_iota(jnp.int32, (n_s, both), 0)
    fwd = lax.broadcasted_iota(jnp.int32, (n_s, both), 1) < half
    expo = jnp.where(fwd, tt - 1 - s_idx, s_idx - (tt - 1))
    live = expo >= 0
    p_re, p_im = power(jnp.maximum(expo, 0))
    w_re = jnp.where(live, p_re * f_re - p_im * f_im, 0.0).T
    w_im = jnp.where(live, p_re * f_im + p_im * f_re, 0.0).T
    width = n_s * SSM_GROUP

    def spread(v, pick):
        k = v.shape[1]
        row = lax.broadcasted_iota(jnp.int32, (3 * k, width), 0)
        onehot = (pick(lax.broadcasted_iota(jnp.int32, (3 * k, width), 1)) == row % k).astype(_BF)
        hi = v.astype(_BF)
        rest = v - hi.astype(_F32)
        mid = rest.astype(_BF)
        lo = (rest - mid.astype(_F32)).astype(_BF)
        return jnp.dot(jnp.concatenate([hi, mid, lo], axis=1), onehot, preferred_element_type=_F32)

    def dot3(a, b):
        a_hi, b_hi = a.astype(_BF), b.astype(_BF)
        a_lo, b_lo = (a - a_hi.astype(_F32)).astype(_BF), (b - b_hi.astype(_F32)).astype(_BF)
        d = functools.partial(jnp.dot, preferred_element_type=_F32)
        return d(a_hi, b_hi) + (d(a_hi, b_lo) + d(a_lo, b_hi))

    w_re, w_im = (spread(w, lambda l: l // SSM_GROUP) for w in (w_re, w_im))
    b_re, b_im = (spread(bcol_ref[k], lambda l: l % SSM_GROUP) for k in (0, 1))
    strip = dot3(c_re, w_re * b_re - w_im * b_im) - dot3(c_im, w_re * b_im + w_im * b_re)
    for t in range(tt):
        lo = SSM_GROUP * (tt - 1 - t)
        mt_ref[SSM_GROUP * t:SSM_GROUP * (t + 1), :] = strip[:, lo:lo + CHUNK_COLS].astype(_BF)


def _s5_regroup_kernel(x_ref, mod_ref, o_ref):
    sh = mod_ref[0]
    sc = 1.0 + mod_ref[1]
    cols = x_ref.shape[0] * x_ref.shape[2]
    for tau in range(CHUNK):
        slab = (x_ref[:, tau] * sc[None] + sh[None]).reshape(cols, D_MODEL)
        o_ref[:, SSM_GROUP * tau:SSM_GROUP * (tau + 1), :] = (
            slab.T.astype(_BF).reshape(SSM_GROUPS, SSM_GROUP, cols))


def _s5_chunk_kernel(bsz, n_chunks, z_ref, mt_ref, bt_ref, ct_ref, dec_ref, y_ref,
                     loc_ref, carry_ref, intra_ref):
    half = SSM_STATE
    n_grp = z_ref.shape[0]
    for g in range(n_grp):
        loc_ref[g] = lax.dot_general(z_ref[g], bt_ref[g], (((0,), (0,)), ((), ())),
                                     preferred_element_type=_F32)
    a_re = [jnp.broadcast_to(dec_ref[g, 0:1, :], (bsz, 2 * half)) for g in range(n_grp)]
    a_im = [jnp.broadcast_to(dec_ref[g, 1:2, :], (bsz, 2 * half)) for g in range(n_grp)]
    is_fwd = lax.broadcasted_iota(jnp.int32, (bsz, 2 * half), 1) < half

    def step(k, state):
        rf = k * bsz
        rb = (n_chunks - 1 - k) * bsz
        new = []
        for g in range(n_grp):
            s_re, s_im = state[2 * g], state[2 * g + 1]
            carry_ref[g, pl.ds(rf, bsz), 0:half] = s_re[:, 0:half]
            carry_ref[g, pl.ds(rb, bsz), half:2 * half] = s_re[:, half:]
            carry_ref[g, pl.ds(rf, bsz), 2 * half:3 * half] = s_im[:, 0:half]
            carry_ref[g, pl.ds(rb, bsz), 3 * half:] = s_im[:, half:]
            l_re = jnp.where(is_fwd, loc_ref[g, pl.ds(rf, bsz), 0:2 * half],
                             loc_ref[g, pl.ds(rb, bsz), 0:2 * half])
            l_im = jnp.where(is_fwd, loc_ref[g, pl.ds(rf, bsz), 2 * half:],
                             loc_ref[g, pl.ds(rb, bsz), 2 * half:])
            new.append(a_re[g] * s_re - a_im[g] * s_im + l_re)
            new.append(a_re[g] * s_im + a_im[g] * s_re + l_im)
        return tuple(new)

    for g in range(n_grp):
        intra_ref[g] = jnp.dot(mt_ref[g], z_ref[g], preferred_element_type=_F32)
    state = (jnp.zeros((bsz, 2 * half), _F32),) * (2 * n_grp)
    for k in range(n_chunks):
        state = step(k, state)
    for g in range(n_grp):
        y = intra_ref[g] + lax.dot_general(ct_ref[g], carry_ref[g].astype(_BF),
                                           (((1,), (1,)), ((), ())), preferred_element_type=_F32)
        y_ref[g] = y.astype(_BF)


def _s5_glu_kernel(x_ref, yt_ref, mod_ref, d_ref, wglu_ref, g_ref, b_ref, o_ref):
    sh = mod_ref[0]
    sc = 1.0 + mod_ref[1]
    gate = 1.0 + mod_ref[2]
    ncb, ntok, bsz, _ = x_ref.shape
    cols = ncb * bsz
    for t0 in range(0, ntok, GLU_SUB):
        zs = []
        for tq in range(t0, t0 + GLU_SUB):
            yd = yt_ref[:, SSM_GROUP * tq:SSM_GROUP * (tq + 1), :].astype(_F32)
            yd = yd.reshape(D_MODEL, cols).T.reshape(ncb, bsz, D_MODEL)
            u = x_ref[:, tq] * sc[None] + sh[None]
            zs.append(_gelu(d_ref[...] * u + yd).reshape(cols, D_MODEL).astype(_BF))
        pz = jnp.dot(jnp.concatenate(zs, axis=0), wglu_ref[...], preferred_element_type=_F32)
        m = pz[:, :D_MODEL] * jax.nn.sigmoid(pz[:, D_MODEL:])
        for k in range(GLU_SUB):
            mt = m[cols * k:cols * (k + 1)].reshape(ncb, bsz, D_MODEL)
            h = DEEPNORM_ALPHA * x_ref[:, t0 + k] + gate[None] * mt
            o_ref[:, t0 + k] = _layer_norm(h, g_ref[...], b_ref[...])


def _s5_mixer_layer(xt, mod, p, i):
    j = i // 2
    seq_len, bsz, _ = xt.shape
    n_chunks = seq_len // CHUNK
    ncb = LANES // bsz
    cols = n_chunks * bsz
    x4 = xt.reshape(n_chunks, CHUNK, bsz, D_MODEL)
    mod_full = pl.BlockSpec((6, bsz, D_MODEL), lambda *_: (0, 0, 0))
    zt = pl.pallas_call(
        _s5_regroup_kernel,
        grid=(n_chunks // ncb,),
        in_specs=[pl.BlockSpec((ncb, CHUNK, bsz, D_MODEL), lambda i: (i, 0, 0, 0)), mod_full],
        out_specs=pl.BlockSpec((SSM_GROUPS, CHUNK_COLS, LANES), lambda i: (0, 0, i)),
        out_shape=jax.ShapeDtypeStruct((SSM_GROUPS, CHUNK_COLS, cols), _BF),
        compiler_params=_params("parallel"),
        name="s5_regroup",
    )(x4, mod)
    gs = S5_GROUPS_PER_STEP
    grp = lambda r, c: pl.BlockSpec((gs, r, c), lambda g: (g, 0, 0))
    lay0 = j * (SSM_GROUPS // gs)
    tab = lambda r, c: pl.BlockSpec((gs, r, c), lambda g: (lay0 + g, 0, 0))
    yt = pl.pallas_call(
        functools.partial(_s5_chunk_kernel, bsz, n_chunks),
        grid=(SSM_GROUPS // gs,),
        in_specs=[grp(CHUNK_COLS, cols), tab(CHUNK_COLS, CHUNK_COLS), tab(CHUNK_COLS, CHUNK_COLS),
                  tab(CHUNK_COLS, CHUNK_COLS), tab(2, 2 * SSM_STATE)],
        out_specs=grp(CHUNK_COLS, cols),
        out_shape=jax.ShapeDtypeStruct((SSM_GROUPS, CHUNK_COLS, cols), _BF),
        scratch_shapes=[pltpu.VMEM((gs, cols, CHUNK_COLS), _F32),
                        pltpu.VMEM((gs, cols, CHUNK_COLS), _F32),
                        pltpu.VMEM((gs, CHUNK_COLS, cols), _F32)],
        compiler_params=_params("parallel"),
        name="s5_chunk_scan",
    )(zt, *p["s5_tables"])
    xblk = pl.BlockSpec((ncb, GLU_TOKS, bsz, D_MODEL), lambda i, q: (i, q, 0, 0))
    out = pl.pallas_call(
        _s5_glu_kernel,
        grid=(n_chunks // ncb, CHUNK // GLU_TOKS),
        in_specs=[xblk,
                  pl.BlockSpec((SSM_GROUPS, SSM_GROUP * GLU_TOKS, LANES), lambda i, q: (0, q, i)),
                  mod_full, _layer_spec(j, (1, D_MODEL)), _layer_spec(j, (D_MODEL, 2 * D_MODEL)),
                  _layer_spec(i, (1, D_MODEL)), _layer_spec(i, (1, D_MODEL))],
        out_specs=xblk,
        out_shape=jax.ShapeDtypeStruct(x4.shape, _F32),
        compiler_params=_params("parallel", "parallel"),
        name="s5_glu_layer",
    )(x4, yt, mod, p["s5_d"], p["s5_w_glu"], p["ln1_g"], p["ln1_b"])
    return out.reshape(xt.shape)


def _trunk(x, mod, p):
    xt = x
    for i in range(DEPTH):
        if i % 2 == 0:
            xt = _conv_mixer_layer(xt, mod[i], p, i, seq_major_in=(i == 0))
        else:
            xt = _s5_mixer_layer(xt, mod[i], p, i)
        xt = _ffn_layer(xt, mod[i], p, i, seq_major_out=(i == DEPTH - 1))
    return xt


def kernel(x_prompt, x_sample, c_prompt, c_sample, ada_w, ada_b, ln1_g, ln1_b, ln2_g, ln2_b, sc_w_in, sc_conv_w, sc_conv_b, sc_w_out, s5_a_re, s5_a_im, s5_log_dt, s5_b_re, s5_b_im, s5_c_re, s5_c_im, s5_d, s5_w_glu, ffn_w_up, ffn_conv_w, ffn_conv_b, ffn_w_down):
    row = lambda v: v[:, None, :]
    p = dict(ln1_g=row(ln1_g), ln1_b=row(ln1_b), ln2_g=row(ln2_g), ln2_b=row(ln2_b),
             sc_w_in=sc_w_in.astype(_BF), sc_conv_w=sc_conv_w, sc_conv_b=row(sc_conv_b),
             sc_w_out=sc_w_out.astype(_BF), s5_d=row(s5_d), s5_w_glu=s5_w_glu.astype(_BF),
             ffn_w_up=ffn_w_up.astype(_BF), ffn_conv_w=ffn_conv_w, ffn_conv_b=row(ffn_conv_b),
             ffn_w_down=ffn_w_down.astype(_BF),
             s5_tables=_s5_tables(s5_a_re, s5_a_im, s5_log_dt, s5_b_re, s5_b_im, s5_c_re, s5_c_im))
    n_prompt = c_prompt.shape[0]
    mod = _ada(jnp.concatenate([c_prompt, c_sample], axis=0), ada_w, ada_b)
    y_prompt = _trunk(x_prompt, mod[:, :, :n_prompt], p)
    y_sample = _trunk(x_sample, mod[:, :, n_prompt:], p)
    return (y_prompt, y_sample)
```

```python
import functools
import math

import jax
import jax.numpy as jnp
from jax import lax
from jax.experimental import pallas as pl
from jax.experimental.pallas import tpu as pltpu

D_MODEL = 1024
DEPTH = 4
FFN_HIDDEN = 2816
SSM_GROUP = 16
SSM_GROUPS = D_MODEL // SSM_GROUP
SSM_STATE = 64
DEEPNORM_ALPHA = (2 * DEPTH) ** 0.25
LN_EPS = 1e-5

SUBLANES = 8
LANES = 128
CHUNK = 16
CHUNK_COLS = CHUNK * SSM_GROUP
assert CHUNK & (CHUNK - 1) == 0
TOK_TILE = 128
GLU_TOKS = 16
GLU_SUB = 2
S5_GROUPS_PER_STEP = 2
TABLE_GROUPS_PER_STEP = 4
FFN_COL_CHUNKS = (1280, 1536)
ROW_SPLITS = 4
VMEM_LIMIT = 56 * 1024 * 1024

_BF = jnp.bfloat16
_F32 = jnp.float32


def _layer_spec(idx, shape):
    nd = len(shape)
    return pl.BlockSpec((None,) + shape, lambda *_: (idx,) + (0,) * nd, pipeline_mode=pl.Buffered(1))


def _params(*sem):
    return pltpu.CompilerParams(dimension_semantics=sem, vmem_limit_bytes=VMEM_LIMIT)


_GELU_C1 = math.sqrt(2.0 / math.pi)
_GELU_C2 = _GELU_C1 * 0.044715


def _gelu_half_gate(x):
    return 0.5 + 0.5 * jnp.tanh(x * (_GELU_C1 + _GELU_C2 * (x * x)))


def _gelu(x):
    return x * _gelu_half_gate(x)


def _gelu_times(x, v):
    return (x * v) * _gelu_half_gate(x)


def _dot_bf16x3(a, b):
    a_hi, b_hi = a.astype(_BF), b.astype(_BF)
    a_lo, b_lo = (a - a_hi.astype(_F32)).astype(_BF), (b - b_hi.astype(_F32)).astype(_BF)
    d = functools.partial(jnp.dot, preferred_element_type=_F32)
    return d(a_hi, b_hi) + (d(a_hi, b_lo) + d(a_lo, b_hi))


def _layer_norm(h, g, b):
    mu = jnp.mean(h, axis=-1, keepdims=True)
    d = h - mu
    var = jnp.mean(d * d, axis=-1, keepdims=True)
    return d * lax.rsqrt(var + LN_EPS) * g + b


def _conv3_rows(cur, before, after, w, b):
    up = jnp.concatenate([before, cur[:-SUBLANES]], axis=0)
    dn = jnp.concatenate([cur[SUBLANES:], after], axis=0)
    return up * w[0:1, :] + cur * w[1:2, :] + dn * w[2:3, :] + b


def _ada_kernel(c_ref, w_ref, b_ref, o_ref):
    c = c_ref[...]
    ca = c * jax.nn.sigmoid(c)
    o_ref[...] = _dot_bf16x3(ca, w_ref[...]) + b_ref[...]


def _ada(c, ada_w, ada_b):
    bsz = c.shape[0]
    return pl.pallas_call(
        _ada_kernel,
        grid=(DEPTH, 6),
        in_specs=[
            pl.BlockSpec((bsz, D_MODEL), lambda i, n: (0, 0)),
            pl.BlockSpec((None, D_MODEL, D_MODEL), lambda i, n: (i, 0, n)),
            pl.BlockSpec((None, 1, D_MODEL), lambda i, n: (i, 0, n)),
        ],
        out_specs=pl.BlockSpec((None, None, bsz, D_MODEL), lambda i, n: (i, n, 0, 0)),
        out_shape=jax.ShapeDtypeStruct((DEPTH, 6, bsz, D_MODEL), _F32),
        compiler_params=_params("arbitrary", "arbitrary"),
        name="ada_mod",
    )(c, ada_w, ada_b.reshape(DEPTH, 1, 6 * D_MODEL))


def _tile_specs(seq_len):
    main = pl.BlockSpec((TOK_TILE, SUBLANES, D_MODEL), lambda b, t: (t, b, 0))
    prev = pl.BlockSpec((1, SUBLANES, D_MODEL),
                        lambda b, t: (jnp.maximum(t * TOK_TILE - 1, 0), b, 0))
    nxt = pl.BlockSpec((1, SUBLANES, D_MODEL),
                       lambda b, t: (jnp.minimum((t + 1) * TOK_TILE, seq_len - 1), b, 0))
    return main, prev, nxt


def _mod_spec():
    return pl.BlockSpec((6, SUBLANES, D_MODEL), lambda b, t: (0, b, 0))


def _modulated_rows(xa, sc, sh):
    u = xa * sc[None] + sh[None]
    return u.reshape(xa.shape[0] * SUBLANES, D_MODEL).astype(_BF)


def _linear_step():
    return pl.program_id(0) * pl.num_programs(1) + pl.program_id(1)


def _gather_copies(x_hbm, buf, sem, slot, step, seq_len):
    tiles = seq_len // TOK_TILE
    seq0 = (step // tiles) * SUBLANES
    t0 = (step % tiles) * TOK_TILE
    spans = ((jnp.maximum(t0 - 1, 0), 1, 0), (t0, TOK_TILE, 1),
             (jnp.minimum(t0 + TOK_TILE, seq_len - 1), 1, TOK_TILE + 1))
    return [pltpu.make_async_copy(x_hbm.at[seq0 + b, pl.ds(src, n), :],
                                  buf.at[slot, pl.ds(dst, n), b, :], sem.at[slot])
            for b in range(SUBLANES) for src, n, dst in spans]


def _scatter_copies(buf, o_hbm, sem, slot, step, seq_len):
    tiles = seq_len // TOK_TILE
    seq0 = (step // tiles) * SUBLANES
    t0 = (step % tiles) * TOK_TILE
    return [pltpu.make_async_copy(buf.at[slot, :, b, :], o_hbm.at[seq0 + b, pl.ds(t0, TOK_TILE), :],
                                  sem.at[slot])
            for b in range(SUBLANES)]


def _gathered_tile(x_hbm, buf, sem, seq_len, n_steps):
    step = _linear_step()
    slot = step % 2

    @pl.when(step == 0)
    def _():
        for c in _gather_copies(x_hbm, buf, sem, slot, step, seq_len):
            c.start()

    @pl.when(step + 1 < n_steps)
    def _():
        for c in _gather_copies(x_hbm, buf, sem, 1 - slot, step + 1, seq_len):
            c.start()

    for c in _gather_copies(x_hbm, buf, sem, slot, step, seq_len):
        c.wait()
    return buf[slot]


def _scatter_slot(buf, o_hbm, sem, seq_len):
    step = _linear_step()
    slot = step % 2

    @pl.when(step >= 2)
    def _():
        for c in _scatter_copies(buf, o_hbm, sem, slot, step - 2, seq_len):
            c.wait()

    return buf.at[slot]


def _scatter_start(buf, o_hbm, sem, seq_len, n_steps):
    step = _linear_step()
    slot = step % 2
    for c in _scatter_copies(buf, o_hbm, sem, slot, step, seq_len):
        c.start()

    @pl.when(step == n_steps - 1)
    def _():
        if n_steps > 1:
            for c in _scatter_copies(buf, o_hbm, sem, 1 - slot, step - 1, seq_len):
                c.wait()
        for c in _scatter_copies(buf, o_hbm, sem, slot, step, seq_len):
            c.wait()


def _conv3_tile(full, first, last, w, b):
    n = full.shape[0]
    zero = jnp.zeros((SUBLANES, full.shape[1]), _F32)
    return _conv3_rows(full[SUBLANES:n - SUBLANES], jnp.where(first, zero, full[:SUBLANES]),
                       jnp.where(last, zero, full[n - SUBLANES:]), w, b)


def _conv_mixer_kernel(gather, *refs):
    if gather:
        x_hbm, mod_ref, win_ref, cw_ref, cb_ref, wout_ref, g_ref, b_ref, o_ref, buf, sem = refs
        xa = _gathered_tile(x_hbm, buf, sem, *gather)
    else:
        (x_ref, xp_ref, xn_ref, mod_ref, win_ref, cw_ref, cb_ref, wout_ref, g_ref, b_ref,
         o_ref) = refs
        xa = jnp.concatenate([xp_ref[...], x_ref[...], xn_ref[...]], axis=0)
    t = pl.program_id(1)
    first = t == 0
    last = t == pl.num_programs(1) - 1
    sh = mod_ref[0]
    sc = 1.0 + mod_ref[1]
    gate = 1.0 + mod_ref[2]
    x = xa[1:xa.shape[0] - 1]
    u = _modulated_rows(xa, sc, sh)
    p = jnp.dot(u, win_ref[...], preferred_element_type=_F32)
    q = p[:, D_MODEL:2 * D_MODEL] * p[:, 2 * D_MODEL:]
    cq = _conv3_tile(q, first, last, cw_ref[...], cb_ref[...])
    r = (p[SUBLANES:p.shape[0] - SUBLANES, :D_MODEL] * cq).astype(_BF)
    tok = x.shape[0] // ROW_SPLITS
    rows = tok * SUBLANES
    for i in range(ROW_SPLITS):
        xr = x[i * tok:(i + 1) * tok]
        m = jnp.dot(r[i * rows:(i + 1) * rows], wout_ref[...], preferred_element_type=_F32)
        h = DEEPNORM_ALPHA * xr + gate[None] * m.reshape(xr.shape)
        o_ref[i * tok:(i + 1) * tok] = _layer_norm(h, g_ref[...], b_ref[...])


def _conv_mixer_layer(x, mod, p, i, seq_major_in=False):
    j = i // 2
    if seq_major_in:
        bsz, seq_len, _ = x.shape
    else:
        seq_len, bsz, _ = x.shape
    grid = (bsz // SUBLANES, seq_len // TOK_TILE)
    main, prev, nxt = _tile_specs(seq_len)
    weights = [_layer_spec(j, (D_MODEL, 3 * D_MODEL)), _layer_spec(j, (3, D_MODEL)),
               _layer_spec(j, (1, D_MODEL)), _layer_spec(j, (D_MODEL, D_MODEL)),
               _layer_spec(i, (1, D_MODEL)), _layer_spec(i, (1, D_MODEL))]
    if seq_major_in:
        gather = (seq_len, grid[0] * grid[1])
        x_specs, x_args = [pl.BlockSpec(memory_space=pl.ANY)], (x,)
        scratch = [pltpu.VMEM((2, TOK_TILE + 2, SUBLANES, D_MODEL), _F32),
                   pltpu.SemaphoreType.DMA((2,))]
        sem = ("arbitrary", "arbitrary")
    else:
        gather = None
        x_specs, x_args = [main, prev, nxt], (x, x, x)
        scratch = []
        sem = ("parallel", "parallel")
    return pl.pallas_call(
        functools.partial(_conv_mixer_kernel, gather),
        grid=grid,
        in_specs=x_specs + [_mod_spec()] + weights,
        out_specs=main,
        out_shape=jax.ShapeDtypeStruct((seq_len, bsz, D_MODEL), _F32),
        scratch_shapes=scratch,
        compiler_params=_params(*sem),
        name="conv_mixer_layer",
    )(*x_args, mod, p["sc_w_in"], p["sc_conv_w"], p["sc_conv_b"], p["sc_w_out"],
      p["ln1_g"], p["ln1_b"])


def _ffn_kernel(scatter, x_ref, xp_ref, xn_ref, mod_ref, wup_ref, cw_ref, cb_ref, wdn_ref, g_ref,
                b_ref, o_ref, *dma_scratch):
    dst_ref = _scatter_slot(dma_scratch[0], o_ref, dma_scratch[1], scatter[0]) if scatter else o_ref
    t = pl.program_id(1)
    first = t == 0
    last = t == pl.num_programs(1) - 1
    sh = mod_ref[3]
    sc = 1.0 + mod_ref[4]
    gate = 1.0 + mod_ref[5]
    x = x_ref[...]
    u = _modulated_rows(jnp.concatenate([xp_ref[...], x, xn_ref[...]], axis=0), sc, sh)
    tok = x.shape[0] // ROW_SPLITS
    rows = tok * SUBLANES
    accs = [None] * ROW_SPLITS
    k0 = 0
    for kc in FFN_COL_CHUNKS:
        halves = []
        for base in (0, FFN_HIDDEN):
            lo = base + k0
            a = jnp.dot(u, wup_ref[:, lo:lo + kc], preferred_element_type=_F32)
            halves.append(_conv3_tile(a, first, last, cw_ref[:, lo:lo + kc], cb_ref[:, lo:lo + kc]))
        hid = _gelu_times(halves[0], halves[1]).astype(_BF)
        for r in range(ROW_SPLITS):
            part = jnp.dot(hid[r * rows:(r + 1) * rows], wdn_ref[k0:k0 + kc, :],
                           preferred_element_type=_F32)
            accs[r] = part if accs[r] is None else accs[r] + part
        k0 += kc
    for r in range(ROW_SPLITS):
        xr = x[r * tok:(r + 1) * tok]
        h = DEEPNORM_ALPHA * xr + gate[None] * accs[r].reshape(xr.shape)
        dst_ref[r * tok:(r + 1) * tok] = _layer_norm(h, g_ref[...], b_ref[...])
    if scatter:
        _scatter_start(dma_scratch[0], o_ref, dma_scratch[1], *scatter)


def _ffn_layer(xt, mod, p, i, seq_major_out=False):
    seq_len, bsz, _ = xt.shape
    grid = (bsz // SUBLANES, seq_len // TOK_TILE)
    main, prev, nxt = _tile_specs(seq_len)
    if seq_major_out:
        scatter = (seq_len, grid[0] * grid[1])
        out_spec = pl.BlockSpec(memory_space=pl.ANY)
        out_shape = (bsz, seq_len, D_MODEL)
        scratch = [pltpu.VMEM((2, TOK_TILE, SUBLANES, D_MODEL), _F32), pltpu.SemaphoreType.DMA((2,))]
        sem = ("arbitrary", "arbitrary")
    else:
        scatter = None
        out_spec, out_shape, scratch = main, xt.shape, []
        sem = ("parallel", "parallel")
    return pl.pallas_call(
        functools.partial(_ffn_kernel, scatter),
        grid=grid,
        in_specs=[main, prev, nxt, _mod_spec(),
                  _layer_spec(i, (D_MODEL, 2 * FFN_HIDDEN)), _layer_spec(i, (3, 2 * FFN_HIDDEN)),
                  _layer_spec(i, (1, 2 * FFN_HIDDEN)), _layer_spec(i, (FFN_HIDDEN, D_MODEL)),
                  _layer_spec(i, (1, D_MODEL)), _layer_spec(i, (1, D_MODEL))],
        out_specs=out_spec,
        out_shape=jax.ShapeDtypeStruct(out_shape, _F32),
        scratch_shapes=scratch,
        compiler_params=_params(*sem),
        name="conv_ffn_layer",
    )(xt, xt, xt, mod, p["ffn_w_up"], p["ffn_conv_w"], p["ffn_conv_b"], p["ffn_w_down"],
      p["ln2_g"], p["ln2_b"])


def _s5_tables(a_re, a_im, log_dt, b_re, b_im, c_re, c_im):
    n_lay, n_dir, n_grp, n_st = a_re.shape
    both = n_dir * n_st
    rows = n_lay * n_grp

    def states_last(v):
        return v.transpose(0, 2, 3, 1, 4).reshape(rows, v.shape[3], both)

    ldt = jnp.broadcast_to(log_dt[..., None], a_re.shape)
    par = states_last(jnp.stack([a_re, a_im, ldt], axis=3))
    b_col = jnp.stack([b_re, b_im], axis=2)
    b_col = b_col.transpose(0, 3, 2, 1, 4, 5).reshape(rows, 2, both, SSM_GROUP)
    b_row = jnp.stack([states_last(b_re.transpose(0, 1, 2, 4, 3)),
                       states_last(b_im.transpose(0, 1, 2, 4, 3))], axis=1)
    c_row = jnp.stack([states_last(c_re), states_last(c_im)], axis=1)
    blk = lambda *shape: pl.BlockSpec((TABLE_GROUPS_PER_STEP,) + shape,
                                      lambda g: (g,) + (0,) * len(shape))
    sq = (CHUNK_COLS, CHUNK_COLS)
    return pl.pallas_call(
        _s5_table_kernel,
        grid=(rows // TABLE_GROUPS_PER_STEP,),
        in_specs=[blk(3, both), blk(2, both, SSM_GROUP), blk(2, SSM_GROUP, both),
                  blk(2, SSM_GROUP, both)],
        out_specs=[blk(*sq), blk(*sq), blk(*sq), blk(2, both)],
        out_shape=[jax.ShapeDtypeStruct((rows,) + sq, _BF)] * 3
        + [jax.ShapeDtypeStruct((rows, 2, both), _F32)],
        compiler_params=_params("parallel"),
        name="s5_tables",
    )(par, b_col, b_row, c_row)


def _s5_table_kernel(*refs):
    for g in range(TABLE_GROUPS_PER_STEP):
        _s5_group_tables(*(r.at[g] for r in refs))


def _s5_group_tables(par_ref, bcol_ref, brow_ref, crow_ref, mt_ref, bt_ref, ct_ref, dec_ref):
    tt = CHUNK
    half = SSM_STATE
    both = 2 * half

    def discretise(a_re, a_im, log_dt):
        dt = jnp.exp(log_dt)
        mag = jnp.exp(a_re * dt)
        lb_re, lb_im = mag * jnp.cos(a_im * dt), mag * jnp.sin(a_im * dt)
        den = a_re * a_re + a_im * a_im
        f_re = ((lb_re - 1.0) * a_re + lb_im * a_im) / den
        f_im = (lb_im * a_re - (lb_re - 1.0) * a_im) / den
        squares = [(lb_re, lb_im)]
        while len(squares) <= tt.bit_length() - 1:
            r, i = squares[-1]
            squares.append((r * r - i * i, 2.0 * (r * i)))

        def power(n):
            p_re = jnp.ones(n.shape, _F32)
            p_im = jnp.zeros(n.shape, _F32)
            for k, (s_re, s_im) in enumerate(squares):
                bit = ((n >> k) & 1) == 1
                p_re, p_im = (jnp.where(bit, p_re * s_re - p_im * s_im, p_re),
                              jnp.where(bit, p_re * s_im + p_im * s_re, p_im))
            return p_re, p_im

        return power, squares[-1], f_re, f_im

    power, lb_chunk, f_re, f_im = discretise(par_ref[0:1, :], par_ref[1:2, :], par_ref[2:3, :])
    tok = lax.broadcasted_iota(jnp.int32, (tt, both), 0)
    fwd = lax.broadcasted_iota(jnp.int32, (tt, both), 1) < half
    p_re, p_im = power(jnp.where(fwd, tt - 1 - tok, tok))
    w_re = (p_re * f_re - p_im * f_im)[:, None, :]
    w_im = (p_re * f_im + p_im * f_re)[:, None, :]
    b_re, b_im = brow_ref[0][None], brow_ref[1][None]
    bt_ref[:, 0:both] = (w_re * b_re - w_im * b_im).reshape(CHUNK_COLS, both).astype(_BF)
    bt_ref[:, both:] = (w_re * b_im + w_im * b_re).reshape(CHUNK_COLS, both).astype(_BF)
    e_re, e_im = power(jnp.where(fwd, tok + 1, tt - tok))
    e_re, e_im = e_re[:, None, :], e_im[:, None, :]
    c_re, c_im = crow_ref[0], crow_ref[1]
    ct_ref[:, 0:both] = (c_re[None] * e_re - c_im[None] * e_im).reshape(CHUNK_COLS, both).astype(_BF)
    ct_ref[:, both:] = (-(c_re[None] * e_im + c_im[None] * e_re)).reshape(CHUNK_COLS, both).astype(_BF)
    dec_ref[0:1, :] = lb_chunk[0]
    dec_ref[1:2, :] = lb_chunk[1]

    n_s = 2 * tt
    s_idx = lax.broadcasted_iota(jnp.int32, (n_s, both), 0)
    fwd = lax.broadcasted_iota(jnp.int32, (n_s, both), 1) < half
    expo = jnp.where(fwd, tt - 1 - s_idx, s_idx - (tt - 1))
    live = expo >= 0
    p_re, p_im = power(jnp.maximum(expo, 0))
    w_re = jnp.where(live, p_re * f_re - p_im * f_im, 0.0).T
    w_im = jnp.where(live, p_re * f_im + p_im * f_re, 0.0).T
    width = n_s * SSM_GROUP

    def spread(v, pick):
        k = v.shape[1]
        row = lax.broadcasted_iota(jnp.int32, (3 * k, width), 0)
        onehot = (pick(lax.broadcasted_iota(jnp.int32, (3 * k, width), 1)) == row % k).astype(_BF)
        hi = v.astype(_BF)
        rest = v - hi.astype(_F32)
        mid = rest.astype(_BF)
        lo = (rest - mid.astype(_F32)).astype(_BF)
        return jnp.dot(jnp.concatenate([hi, mid, lo], axis=1), onehot, preferred_element_type=_F32)

    w_re, w_im = (spread(w, lambda l: l // SSM_GROUP) for w in (w_re, w_im))
    b_re, b_im = (spread(bcol_ref[k], lambda l: l % SSM_GROUP) for k in (0, 1))
    strip = (_dot_bf16x3(c_re, w_re * b_re - w_im * b_im)
             - _dot_bf16x3(c_im, w_re * b_im + w_im * b_re))
    for t in range(tt):
        lo = SSM_GROUP * (tt - 1 - t)
        mt_ref[SSM_GROUP * t:SSM_GROUP * (t + 1), :] = strip[:, lo:lo + CHUNK_COLS].astype(_BF)


def _s5_regroup_kernel(x_ref, mod_ref, o_ref):
    sh = mod_ref[0]
    sc = 1.0 + mod_ref[1]
    cols = x_ref.shape[0] * x_ref.shape[2]
    for tau in range(CHUNK):
        slab = (x_ref[:, tau] * sc[None] + sh[None]).reshape(cols, D_MODEL)
        o_ref[:, SSM_GROUP * tau:SSM_GROUP * (tau + 1), :] = (
            slab.T.astype(_BF).reshape(SSM_GROUPS, SSM_GROUP, cols))


def _s5_chunk_kernel(bsz, n_chunks, z_ref, mt_ref, bt_ref, ct_ref, dec_ref, y_ref,
                     loc_ref, carry_ref, intra_ref):
    half = SSM_STATE
    n_grp = z_ref.shape[0]
    for g in range(n_grp):
        loc_ref[g] = lax.dot_general(z_ref[g], bt_ref[g], (((0,), (0,)), ((), ())),
                                     preferred_element_type=_F32)
    a_re = [jnp.broadcast_to(dec_ref[g, 0:1, :], (bsz, 2 * half)) for g in range(n_grp)]
    a_im = [jnp.broadcast_to(dec_ref[g, 1:2, :], (bsz, 2 * half)) for g in range(n_grp)]
    is_fwd = lax.broadcasted_iota(jnp.int32, (bsz, 2 * half), 1) < half

    def step(k, state):
        rf = k * bsz
        rb = (n_chunks - 1 - k) * bsz
        new = []
        for g in range(n_grp):
            s_re, s_im = state[2 * g], state[2 * g + 1]
            carry_ref[g, pl.ds(rf, bsz), 0:half] = s_re[:, 0:half]
            carry_ref[g, pl.ds(rb, bsz), half:2 * half] = s_re[:, half:]
            carry_ref[g, pl.ds(rf, bsz), 2 * half:3 * half] = s_im[:, 0:half]
            carry_ref[g, pl.ds(rb, bsz), 3 * half:] = s_im[:, half:]
            l_re = jnp.where(is_fwd, loc_ref[g, pl.ds(rf, bsz), 0:2 * half],
                             loc_ref[g, pl.ds(rb, bsz), 0:2 * half])
            l_im = jnp.where(is_fwd, loc_ref[g, pl.ds(rf, bsz), 2 * half:],
                             loc_ref[g, pl.ds(rb, bsz), 2 * half:])
            new.append(a_re[g] * s_re - a_im[g] * s_im + l_re)
            new.append(a_re[g] * s_im + a_im[g] * s_re + l_im)
        return tuple(new)

    for g in range(n_grp):
        intra_ref[g] = jnp.dot(mt_ref[g], z_ref[g], preferred_element_type=_F32)
    state = (jnp.zeros((bsz, 2 * half), _F32),) * (2 * n_grp)
    for k in range(n_chunks):
        state = step(k, state)
    for g in range(n_grp):
        y = intra_ref[g] + lax.dot_general(ct_ref[g], carry_ref[g].astype(_BF),
                                           (((1,), (1,)), ((), ())), preferred_element_type=_F32)
        y_ref[g] = y.astype(_BF)


def _s5_glu_kernel(x_ref, yt_ref, mod_ref, d_ref, wglu_ref, g_ref, b_ref, o_ref):
    sh = mod_ref[0]
    sc = 1.0 + mod_ref[1]
    gate = 1.0 + mod_ref[2]
    ncb, ntok, bsz, _ = x_ref.shape
    cols = ncb * bsz
    d_sc, d_sh = d_ref[...] * sc, d_ref[...] * sh
    for t0 in range(0, ntok, GLU_SUB):
        zs = []
        for tq in range(t0, t0 + GLU_SUB):
            yd = yt_ref[:, SSM_GROUP * tq:SSM_GROUP * (tq + 1), :].astype(_F32)
            yd = yd.reshape(D_MODEL, cols).T.reshape(ncb, bsz, D_MODEL)
            y = x_ref[:, tq] * d_sc[None] + (d_sh[None] + yd)
            zs.append(_gelu(y).reshape(cols, D_MODEL).astype(_BF))
        pz = jnp.dot(jnp.concatenate(zs, axis=0), wglu_ref[...], preferred_element_type=_F32)
        m = pz[:, :D_MODEL] * jax.nn.sigmoid(pz[:, D_MODEL:])
        for k in range(GLU_SUB):
            mt = m[cols * k:cols * (k + 1)].reshape(ncb, bsz, D_MODEL)
            h = DEEPNORM_ALPHA * x_ref[:, t0 + k] + gate[None] * mt
            o_ref[:, t0 + k] = _layer_norm(h, g_ref[...], b_ref[...])


def _s5_mixer_layer(xt, mod, p, i):
    j = i // 2
    seq_len, bsz, _ = xt.shape
    n_chunks = seq_len // CHUNK
    ncb = LANES // bsz
    cols = n_chunks * bsz
    x4 = xt.reshape(n_chunks, CHUNK, bsz, D_MODEL)
    mod_full = pl.BlockSpec((6, bsz, D_MODEL), lambda *_: (0, 0, 0))
    zt = pl.pallas_call(
        _s5_regroup_kernel,
        grid=(n_chunks // ncb,),
        in_specs=[pl.BlockSpec((ncb, CHUNK, bsz, D_MODEL), lambda i: (i, 0, 0, 0)), mod_full],
        out_specs=pl.BlockSpec((SSM_GROUPS, CHUNK_COLS, LANES), lambda i: (0, 0, i)),
        out_shape=jax.ShapeDtypeStruct((SSM_GROUPS, CHUNK_COLS, cols), _BF),
        compiler_params=_params("parallel"),
        name="s5_regroup",
    )(x4, mod)
    gs = S5_GROUPS_PER_STEP
    grp = lambda r, c: pl.BlockSpec((gs, r, c), lambda g: (g, 0, 0))
    lay0 = j * (SSM_GROUPS // gs)
    tab = lambda r, c: pl.BlockSpec((gs, r, c), lambda g: (lay0 + g, 0, 0))
    yt = pl.pallas_call(
        functools.partial(_s5_chunk_kernel, bsz, n_chunks),
        grid=(SSM_GROUPS // gs,),
        in_specs=[grp(CHUNK_COLS, cols), tab(CHUNK_COLS, CHUNK_COLS), tab(CHUNK_COLS, CHUNK_COLS),
                  tab(CHUNK_COLS, CHUNK_COLS), tab(2, 2 * SSM_STATE)],
        out_specs=grp(CHUNK_COLS, cols),
        out_shape=jax.ShapeDtypeStruct((SSM_GROUPS, CHUNK_COLS, cols), _BF),
        scratch_shapes=[pltpu.VMEM((gs, cols, CHUNK_COLS), _F32),
                        pltpu.VMEM((gs, cols, CHUNK_COLS), _F32),
                        pltpu.VMEM((gs, CHUNK_COLS, cols), _F32)],
        compiler_params=_params("parallel"),
        name="s5_chunk_scan",
    )(zt, *p["s5_tables"])
    xblk = pl.BlockSpec((ncb, GLU_TOKS, bsz, D_MODEL), lambda i, q: (i, q, 0, 0))
    out = pl.pallas_call(
        _s5_glu_kernel,
        grid=(n_chunks // ncb, CHUNK // GLU_TOKS),
        in_specs=[xblk,
                  pl.BlockSpec((SSM_GROUPS, SSM_GROUP * GLU_TOKS, LANES), lambda i, q: (0, q, i)),
                  mod_full, _layer_spec(j, (1, D_MODEL)), _layer_spec(j, (D_MODEL, 2 * D_MODEL)),
                  _layer_spec(i, (1, D_MODEL)), _layer_spec(i, (1, D_MODEL))],
        out_specs=xblk,
        out_shape=jax.ShapeDtypeStruct(x4.shape, _F32),
        compiler_params=_params("parallel", "parallel"),
        name="s5_glu_layer",
    )(x4, yt, mod, p["s5_d"], p["s5_w_glu"], p["ln1_g"], p["ln1_b"])
    return out.reshape(xt.shape)


def _trunk(x, mod, p):
    xt = x
    for i in range(DEPTH):
        if i % 2 == 0:
            xt = _conv_mixer_layer(xt, mod[i], p, i, seq_major_in=(i == 0))
        else:
            xt = _s5_mixer_layer(xt, mod[i], p, i)
        xt = _ffn_layer(xt, mod[i], p, i, seq_major_out=(i == DEPTH - 1))
    return xt


def kernel(x_prompt, x_sample, c_prompt, c_sample, ada_w, ada_b, ln1_g, ln1_b, ln2_g, ln2_b, sc_w_in, sc_conv_w, sc_conv_b, sc_w_out, s5_a_re, s5_a_im, s5_log_dt, s5_b_re, s5_b_im, s5_c_re, s5_c_im, s5_d, s5_w_glu, ffn_w_up, ffn_conv_w, ffn_conv_b, ffn_w_down):
    row = lambda v: v[:, None, :]
    p = dict(ln1_g=row(ln1_g), ln1_b=row(ln1_b), ln2_g=row(ln2_g), ln2_b=row(ln2_b),
             sc_w_in=sc_w_in.astype(_BF), sc_conv_w=sc_conv_w, sc_conv_b=row(sc_conv_b),
             sc_w_out=sc_w_out.astype(_BF), s5_d=row(s5_d), s5_w_glu=s5_w_glu.astype(_BF),
             ffn_w_up=ffn_w_up.astype(_BF), ffn_conv_w=ffn_conv_w, ffn_conv_b=row(ffn_conv_b),
             ffn_w_down=ffn_w_down.astype(_BF),
             s5_tables=_s5_tables(s5_a_re, s5_a_im, s5_log_dt, s5_b_re, s5_b_im, s5_c_re, s5_c_im))
    n_prompt = c_prompt.shape[0]
    mod = _ada(jnp.concatenate([c_prompt, c_sample], axis=0), ada_w, ada_b)
    y_prompt = _trunk(x_prompt, mod[:, :, :n_prompt], p)
    y_sample = _trunk(x_sample, mod[:, :, n_prompt:], p)
    return (y_prompt, y_sample)
```

```python
import functools
import math

import jax
import jax.numpy as jnp
from jax import lax
from jax.experimental import pallas as pl
from jax.experimental.pallas import tpu as pltpu

D_MODEL = 1024
DEPTH = 4
FFN_HIDDEN = 2816
SSM_GROUP = 16
SSM_GROUPS = D_MODEL // SSM_GROUP
SSM_STATE = 64
DEEPNORM_ALPHA = (2 * DEPTH) ** 0.25
LN_EPS = 1e-5

SUBLANES = 8
LANES = 128
CHUNK = 16
CHUNK_COLS = CHUNK * SSM_GROUP
assert CHUNK & (CHUNK - 1) == 0
TOK_TILE = 128
GLU_TOKS = 16
GLU_SUB = 2
S5_MAX_GROUPS_PER_STEP = 8
S5_CHUNK_VMEM_BUDGET = 44 * 1024 * 1024
TABLE_GROUPS_PER_STEP = 4
FFN_COL_CHUNKS = (1280, 1536)
ROW_SPLITS = 4
VMEM_LIMIT = 56 * 1024 * 1024

_BF = jnp.bfloat16
_F32 = jnp.float32


def _layer_spec(idx, shape):
    nd = len(shape)
    return pl.BlockSpec((None,) + shape, lambda *_: (idx,) + (0,) * nd, pipeline_mode=pl.Buffered(1))


def _params(*sem):
    return pltpu.CompilerParams(dimension_semantics=sem, vmem_limit_bytes=VMEM_LIMIT)


_GELU_C1 = math.sqrt(2.0 / math.pi)
_GELU_C2 = _GELU_C1 * 0.044715


def _gelu_half_gate(x):
    return 0.5 + 0.5 * jnp.tanh(x * (_GELU_C1 + _GELU_C2 * (x * x)))


def _gelu(x):
    return x * _gelu_half_gate(x)


def _gelu_times(x, v):
    return (x * v) * _gelu_half_gate(x)


def _dot_bf16x3(a, b):
    a_hi, b_hi = a.astype(_BF), b.astype(_BF)
    a_lo, b_lo = (a - a_hi.astype(_F32)).astype(_BF), (b - b_hi.astype(_F32)).astype(_BF)
    d = functools.partial(jnp.dot, preferred_element_type=_F32)
    return d(a_hi, b_hi) + (d(a_hi, b_lo) + d(a_lo, b_hi))


def _residual_layer_norm(x, gate, m, g, b):
    h = x + (gate * (1.0 / DEEPNORM_ALPHA)) * m
    mu = jnp.mean(h, axis=-1, keepdims=True)
    d = h - mu
    var = jnp.mean(d * d, axis=-1, keepdims=True)
    return d * lax.rsqrt(var + LN_EPS / DEEPNORM_ALPHA ** 2) * g + b


def _conv3_rows(cur, before, after, w, b):
    up = jnp.concatenate([before, cur[:-SUBLANES]], axis=0)
    dn = jnp.concatenate([cur[SUBLANES:], after], axis=0)
    return up * w[0:1, :] + cur * w[1:2, :] + dn * w[2:3, :] + b


def _ada_kernel(c_ref, w_ref, b_ref, o_ref):
    c = c_ref[...]
    ca = c * jax.nn.sigmoid(c)
    o_ref[...] = _dot_bf16x3(ca, w_ref[...]) + b_ref[...]


def _ada(c, ada_w, ada_b):
    bsz = c.shape[0]
    return pl.pallas_call(
        _ada_kernel,
        grid=(DEPTH, 6),
        in_specs=[
            pl.BlockSpec((bsz, D_MODEL), lambda i, n: (0, 0)),
            pl.BlockSpec((None, D_MODEL, D_MODEL), lambda i, n: (i, 0, n)),
            pl.BlockSpec((None, 1, D_MODEL), lambda i, n: (i, 0, n)),
        ],
        out_specs=pl.BlockSpec((None, None, bsz, D_MODEL), lambda i, n: (i, n, 0, 0)),
        out_shape=jax.ShapeDtypeStruct((DEPTH, 6, bsz, D_MODEL), _F32),
        compiler_params=_params("arbitrary", "arbitrary"),
        name="ada_mod",
    )(c, ada_w, ada_b.reshape(DEPTH, 1, 6 * D_MODEL))


def _tile_specs(seq_len):
    main = pl.BlockSpec((TOK_TILE, SUBLANES, D_MODEL), lambda b, t: (t, b, 0))
    prev = pl.BlockSpec((1, SUBLANES, D_MODEL),
                        lambda b, t: (jnp.maximum(t * TOK_TILE - 1, 0), b, 0))
    nxt = pl.BlockSpec((1, SUBLANES, D_MODEL),
                       lambda b, t: (jnp.minimum((t + 1) * TOK_TILE, seq_len - 1), b, 0))
    return main, prev, nxt


def _mod_spec():
    return pl.BlockSpec((6, SUBLANES, D_MODEL), lambda b, t: (0, b, 0))


def _modulated_rows(xa, sc, sh):
    u = xa * sc[None] + sh[None]
    return u.reshape(xa.shape[0] * SUBLANES, D_MODEL).astype(_BF)


def _linear_step():
    return pl.program_id(0) * pl.num_programs(1) + pl.program_id(1)


def _gather_copies(x_hbm, buf, sem, slot, step, seq_len):
    tiles = seq_len // TOK_TILE
    seq0 = (step // tiles) * SUBLANES
    t0 = (step % tiles) * TOK_TILE
    spans = ((jnp.maximum(t0 - 1, 0), 1, 0), (t0, TOK_TILE, 1),
             (jnp.minimum(t0 + TOK_TILE, seq_len - 1), 1, TOK_TILE + 1))
    return [pltpu.make_async_copy(x_hbm.at[seq0 + b, pl.ds(src, n), :],
                                  buf.at[slot, pl.ds(dst, n), b, :], sem.at[slot])
            for b in range(SUBLANES) for src, n, dst in spans]


def _scatter_copies(buf, o_hbm, sem, slot, step, seq_len):
    tiles = seq_len // TOK_TILE
    seq0 = (step // tiles) * SUBLANES
    t0 = (step % tiles) * TOK_TILE
    return [pltpu.make_async_copy(buf.at[slot, :, b, :], o_hbm.at[seq0 + b, pl.ds(t0, TOK_TILE), :],
                                  sem.at[slot])
            for b in range(SUBLANES)]


def _gathered_tile(x_hbm, buf, sem, seq_len, n_steps):
    step = _linear_step()
    slot = step % 2

    @pl.when(step == 0)
    def _():
        for c in _gather_copies(x_hbm, buf, sem, slot, step, seq_len):
            c.start()

    @pl.when(step + 1 < n_steps)
    def _():
        for c in _gather_copies(x_hbm, buf, sem, 1 - slot, step + 1, seq_len):
            c.start()

    for c in _gather_copies(x_hbm, buf, sem, slot, step, seq_len):
        c.wait()
    return buf[slot]


def _scatter_slot(buf, o_hbm, sem, seq_len):
    step = _linear_step()
    slot = step % 2

    @pl.when(step >= 2)
    def _():
        for c in _scatter_copies(buf, o_hbm, sem, slot, step - 2, seq_len):
            c.wait()

    return buf.at[slot]


def _scatter_start(buf, o_hbm, sem, seq_len, n_steps):
    step = _linear_step()
    slot = step % 2
    for c in _scatter_copies(buf, o_hbm, sem, slot, step, seq_len):
        c.start()

    @pl.when(step == n_steps - 1)
    def _():
        if n_steps > 1:
            for c in _scatter_copies(buf, o_hbm, sem, 1 - slot, step - 1, seq_len):
                c.wait()
        for c in _scatter_copies(buf, o_hbm, sem, slot, step, seq_len):
            c.wait()


def _conv3_tile(full, first, last, w, b):
    n = full.shape[0]
    zero = jnp.zeros((SUBLANES, full.shape[1]), _F32)
    return _conv3_rows(full[SUBLANES:n - SUBLANES], jnp.where(first, zero, full[:SUBLANES]),
                       jnp.where(last, zero, full[n - SUBLANES:]), w, b)


def _conv_mixer_kernel(gather, *refs):
    if gather:
        x_hbm, mod_ref, win_ref, cw_ref, cb_ref, wout_ref, g_ref, b_ref, o_ref, buf, sem = refs
        xa = _gathered_tile(x_hbm, buf, sem, *gather)
    else:
        (x_ref, xp_ref, xn_ref, mod_ref, win_ref, cw_ref, cb_ref, wout_ref, g_ref, b_ref,
         o_ref) = refs
        xa = jnp.concatenate([xp_ref[...], x_ref[...], xn_ref[...]], axis=0)
    t = pl.program_id(1)
    first = t == 0
    last = t == pl.num_programs(1) - 1
    sh = mod_ref[0]
    sc = 1.0 + mod_ref[1]
    gate = 1.0 + mod_ref[2]
    x = xa[1:xa.shape[0] - 1]
    u = _modulated_rows(xa, sc, sh)
    p = jnp.dot(u, win_ref[...], preferred_element_type=_F32)
    q = p[:, D_MODEL:2 * D_MODEL] * p[:, 2 * D_MODEL:]
    cq = _conv3_tile(q, first, last, cw_ref[...], cb_ref[...])
    r = (p[SUBLANES:p.shape[0] - SUBLANES, :D_MODEL] * cq).astype(_BF)
    tok = x.shape[0] // ROW_SPLITS
    rows = tok * SUBLANES
    for i in range(ROW_SPLITS):
        xr = x[i * tok:(i + 1) * tok]
        m = jnp.dot(r[i * rows:(i + 1) * rows], wout_ref[...], preferred_element_type=_F32)
        o_ref[i * tok:(i + 1) * tok] = _residual_layer_norm(
            xr, gate[None], m.reshape(xr.shape), g_ref[...], b_ref[...])


def _conv_mixer_layer(x, mod, p, i, seq_major_in=False):
    j = i // 2
    if seq_major_in:
        bsz, seq_len, _ = x.shape
    else:
        seq_len, bsz, _ = x.shape
    grid = (bsz // SUBLANES, seq_len // TOK_TILE)
    main, prev, nxt = _tile_specs(seq_len)
    weights = [_layer_spec(j, (D_MODEL, 3 * D_MODEL)), _layer_spec(j, (3, D_MODEL)),
               _layer_spec(j, (1, D_MODEL)), _layer_spec(j, (D_MODEL, D_MODEL)),
               _layer_spec(i, (1, D_MODEL)), _layer_spec(i, (1, D_MODEL))]
    if seq_major_in:
        gather = (seq_len, grid[0] * grid[1])
        x_specs, x_args = [pl.BlockSpec(memory_space=pl.ANY)], (x,)
        scratch = [pltpu.VMEM((2, TOK_TILE + 2, SUBLANES, D_MODEL), _F32),
                   pltpu.SemaphoreType.DMA((2,))]
        sem = ("arbitrary", "arbitrary")
    else:
        gather = None
        x_specs, x_args = [main, prev, nxt], (x, x, x)
        scratch = []
        sem = ("parallel", "parallel")
    return pl.pallas_call(
        functools.partial(_conv_mixer_kernel, gather),
        grid=grid,
        in_specs=x_specs + [_mod_spec()] + weights,
        out_specs=main,
        out_shape=jax.ShapeDtypeStruct((seq_len, bsz, D_MODEL), _F32),
        scratch_shapes=scratch,
        compiler_params=_params(*sem),
        name="conv_mixer_layer",
    )(*x_args, mod, p["sc_w_in"], p["sc_conv_w"], p["sc_conv_b"], p["sc_w_out"],
      p["ln1_g"], p["ln1_b"])


def _ffn_kernel(scatter, x_ref, xp_ref, xn_ref, mod_ref, wup_ref, cw_ref, cb_ref, wdn_ref, g_ref,
                b_ref, o_ref, *dma_scratch):
    dst_ref = _scatter_slot(dma_scratch[0], o_ref, dma_scratch[1], scatter[0]) if scatter else o_ref
    t = pl.program_id(1)
    first = t == 0
    last = t == pl.num_programs(1) - 1
    sh = mod_ref[3]
    sc = 1.0 + mod_ref[4]
    gate = 1.0 + mod_ref[5]
    x = x_ref[...]
    u = _modulated_rows(jnp.concatenate([xp_ref[...], x, xn_ref[...]], axis=0), sc, sh)
    tok = x.shape[0] // ROW_SPLITS
    rows = tok * SUBLANES
    accs = [None] * ROW_SPLITS
    k0 = 0
    for kc in FFN_COL_CHUNKS:
        halves = []
        for base in (0, FFN_HIDDEN):
            lo = base + k0
            a = jnp.dot(u, wup_ref[:, lo:lo + kc], preferred_element_type=_F32)
            halves.append(_conv3_tile(a, first, last, cw_ref[:, lo:lo + kc], cb_ref[:, lo:lo + kc]))
        hid = _gelu_times(halves[0], halves[1]).astype(_BF)
        for r in range(ROW_SPLITS):
            part = jnp.dot(hid[r * rows:(r + 1) * rows], wdn_ref[k0:k0 + kc, :],
                           preferred_element_type=_F32)
            accs[r] = part if accs[r] is None else accs[r] + part
        k0 += kc
    for r in range(ROW_SPLITS):
        xr = x[r * tok:(r + 1) * tok]
        dst_ref[r * tok:(r + 1) * tok] = _residual_layer_norm(
            xr, gate[None], accs[r].reshape(xr.shape), g_ref[...], b_ref[...])
    if scatter:
        _scatter_start(dma_scratch[0], o_ref, dma_scratch[1], *scatter)


def _ffn_layer(xt, mod, p, i, seq_major_out=False):
    seq_len, bsz, _ = xt.shape
    grid = (bsz // SUBLANES, seq_len // TOK_TILE)
    main, prev, nxt = _tile_specs(seq_len)
    if seq_major_out:
        scatter = (seq_len, grid[0] * grid[1])
        out_spec = pl.BlockSpec(memory_space=pl.ANY)
        out_shape = (bsz, seq_len, D_MODEL)
        scratch = [pltpu.VMEM((2, TOK_TILE, SUBLANES, D_MODEL), _F32), pltpu.SemaphoreType.DMA((2,))]
        sem = ("arbitrary", "arbitrary")
    else:
        scatter = None
        out_spec, out_shape, scratch = main, xt.shape, []
        sem = ("parallel", "parallel")
    return pl.pallas_call(
        functools.partial(_ffn_kernel, scatter),
        grid=grid,
        in_specs=[main, prev, nxt, _mod_spec(),
                  _layer_spec(i, (D_MODEL, 2 * FFN_HIDDEN)), _layer_spec(i, (3, 2 * FFN_HIDDEN)),
                  _layer_spec(i, (1, 2 * FFN_HIDDEN)), _layer_spec(i, (FFN_HIDDEN, D_MODEL)),
                  _layer_spec(i, (1, D_MODEL)), _layer_spec(i, (1, D_MODEL))],
        out_specs=out_spec,
        out_shape=jax.ShapeDtypeStruct(out_shape, _F32),
        scratch_shapes=scratch,
        compiler_params=_params(*sem),
        name="conv_ffn_layer",
    )(xt, xt, xt, mod, p["ffn_w_up"], p["ffn_conv_w"], p["ffn_conv_b"], p["ffn_w_down"],
      p["ln2_g"], p["ln2_b"])


def _s5_tables(a_re, a_im, log_dt, b_re, b_im, c_re, c_im):
    n_lay, n_dir, n_grp, n_st = a_re.shape
    both = n_dir * n_st
    rows = n_lay * n_grp

    def states_last(v):
        return v.transpose(0, 2, 3, 1, 4).reshape(rows, v.shape[3], both)

    ldt = jnp.broadcast_to(log_dt[..., None], a_re.shape)
    par = states_last(jnp.stack([a_re, a_im, ldt], axis=3))
    b_col = jnp.stack([b_re, b_im], axis=2)
    b_col = b_col.transpose(0, 3, 2, 1, 4, 5).reshape(rows, 2, both, SSM_GROUP)
    b_row = jnp.stack([states_last(b_re.transpose(0, 1, 2, 4, 3)),
                       states_last(b_im.transpose(0, 1, 2, 4, 3))], axis=1)
    c_row = jnp.stack([states_last(c_re), states_last(c_im)], axis=1)
    blk = lambda *shape: pl.BlockSpec((TABLE_GROUPS_PER_STEP,) + shape,
                                      lambda g: (g,) + (0,) * len(shape))
    sq = (CHUNK_COLS, CHUNK_COLS)
    return pl.pallas_call(
        _s5_table_kernel,
        grid=(rows // TABLE_GROUPS_PER_STEP,),
        in_specs=[blk(3, both), blk(2, both, SSM_GROUP), blk(2, SSM_GROUP, both),
                  blk(2, SSM_GROUP, both)],
        out_specs=[blk(*sq), blk(*sq), blk(*sq), blk(2, both)],
        out_shape=[jax.ShapeDtypeStruct((rows,) + sq, _BF)] * 3
        + [jax.ShapeDtypeStruct((rows, 2, both), _F32)],
        compiler_params=_params("parallel"),
        name="s5_tables",
    )(par, b_col, b_row, c_row)


def _s5_table_kernel(*refs):
    for g in range(TABLE_GROUPS_PER_STEP):
        _s5_group_tables(*(r.at[g] for r in refs))


def _s5_group_tables(par_ref, bcol_ref, brow_ref, crow_ref, mt_ref, bt_ref, ct_ref, dec_ref):
    tt = CHUNK
    half = SSM_STATE
    both = 2 * half

    def discretise(a_re, a_im, log_dt):
        dt = jnp.exp(log_dt)
        mag = jnp.exp(a_re * dt)
        lb_re, lb_im = mag * jnp.cos(a_im * dt), mag * jnp.sin(a_im * dt)
        den = a_re * a_re + a_im * a_im
        f_re = ((lb_re - 1.0) * a_re + lb_im * a_im) / den
        f_im = (lb_im * a_re - (lb_re - 1.0) * a_im) / den
        squares = [(lb_re, lb_im)]
        while len(squares) <= tt.bit_length() - 1:
            r, i = squares[-1]
            squares.append((r * r - i * i, 2.0 * (r * i)))

        def power(n):
            p_re = jnp.ones(n.shape, _F32)
            p_im = jnp.zeros(n.shape, _F32)
            for k, (s_re, s_im) in enumerate(squares):
                bit = ((n >> k) & 1) == 1
                p_re, p_im = (jnp.where(bit, p_re * s_re - p_im * s_im, p_re),
                              jnp.where(bit, p_re * s_im + p_im * s_re, p_im))
            return p_re, p_im

        return power, squares[-1], f_re, f_im

    power, lb_chunk, f_re, f_im = discretise(par_ref[0:1, :], par_ref[1:2, :], par_ref[2:3, :])
    tok = lax.broadcasted_iota(jnp.int32, (tt, both), 0)
    fwd = lax.broadcasted_iota(jnp.int32, (tt, both), 1) < half
    p_re, p_im = power(jnp.where(fwd, tt - 1 - tok, tok))
    w_re = (p_re * f_re - p_im * f_im)[:, None, :]
    w_im = (p_re * f_im + p_im * f_re)[:, None, :]
    b_re, b_im = brow_ref[0][None], brow_ref[1][None]
    bt_ref[:, 0:both] = (w_re * b_re - w_im * b_im).reshape(CHUNK_COLS, both).astype(_BF)
    bt_ref[:, both:] = (w_re * b_im + w_im * b_re).reshape(CHUNK_COLS, both).astype(_BF)
    e_re, e_im = power(jnp.where(fwd, tok + 1, tt - tok))
    e_re, e_im = e_re[:, None, :], e_im[:, None, :]
    c_re, c_im = crow_ref[0], crow_ref[1]
    ct_ref[:, 0:both] = (c_re[None] * e_re - c_im[None] * e_im).reshape(CHUNK_COLS, both).astype(_BF)
    ct_ref[:, both:] = (-(c_re[None] * e_im + c_im[None] * e_re)).reshape(CHUNK_COLS, both).astype(_BF)
    dec_ref[0:1, :] = lb_chunk[0]
    dec_ref[1:2, :] = lb_chunk[1]

    n_s = 2 * tt
    s_idx = lax.broadcasted_iota(jnp.int32, (n_s, both), 0)
    fwd = lax.broadcasted_iota(jnp.int32, (n_s, both), 1) < half
    expo = jnp.where(fwd, tt - 1 - s_idx, s_idx - (tt - 1))
    live = expo >= 0
    p_re, p_im = power(jnp.maximum(expo, 0))
    w_re = jnp.where(live, p_re * f_re - p_im * f_im, 0.0).T
    w_im = jnp.where(live, p_re * f_im + p_im * f_re, 0.0).T
    width = n_s * SSM_GROUP

    def spread(v, pick):
        k = v.shape[1]
        row = lax.broadcasted_iota(jnp.int32, (3 * k, width), 0)
        onehot = (pick(lax.broadcasted_iota(jnp.int32, (3 * k, width), 1)) == row % k).astype(_BF)
        hi = v.astype(_BF)
        rest = v - hi.astype(_F32)
        mid = rest.astype(_BF)
        lo = (rest - mid.astype(_F32)).astype(_BF)
        return jnp.dot(jnp.concatenate([hi, mid, lo], axis=1), onehot, preferred_element_type=_F32)

    w_re, w_im = (spread(w, lambda l: l // SSM_GROUP) for w in (w_re, w_im))
    b_re, b_im = (spread(bcol_ref[k], lambda l: l % SSM_GROUP) for k in (0, 1))
    strip = (_dot_bf16x3(c_re, w_re * b_re - w_im * b_im)
             - _dot_bf16x3(c_im, w_re * b_im + w_im * b_re))
    for t in range(tt):
        lo = SSM_GROUP * (tt - 1 - t)
        mt_ref[SSM_GROUP * t:SSM_GROUP * (t + 1), :] = strip[:, lo:lo + CHUNK_COLS].astype(_BF)


def _s5_regroup_kernel(x_ref, mod_ref, o_ref):
    sh = mod_ref[0]
    sc = 1.0 + mod_ref[1]
    cols = x_ref.shape[0] * x_ref.shape[2]
    for tau in range(CHUNK):
        slab = (x_ref[:, tau] * sc[None] + sh[None]).reshape(cols, D_MODEL)
        o_ref[:, SSM_GROUP * tau:SSM_GROUP * (tau + 1), :] = (
            slab.T.astype(_BF).reshape(SSM_GROUPS, SSM_GROUP, cols))


def _s5_chunk_kernel(bsz, n_chunks, z_ref, mt_ref, bt_ref, ct_ref, dec_ref, y_ref,
                     loc_ref, carry_ref, intra_ref):
    half = SSM_STATE
    n_grp = z_ref.shape[0]
    for g in range(n_grp):
        loc_ref[g] = lax.dot_general(z_ref[g], bt_ref[g], (((0,), (0,)), ((), ())),
                                     preferred_element_type=_F32)
    a_re = [jnp.broadcast_to(dec_ref[g, 0:1, :], (bsz, 2 * half)) for g in range(n_grp)]
    a_im = [jnp.broadcast_to(dec_ref[g, 1:2, :], (bsz, 2 * half)) for g in range(n_grp)]
    is_fwd = lax.broadcasted_iota(jnp.int32, (bsz, 2 * half), 1) < half

    def step(k, state):
        rf = k * bsz
        rb = (n_chunks - 1 - k) * bsz
        new = []
        for g in range(n_grp):
            s_re, s_im = state[2 * g], state[2 * g + 1]
            carry_ref[g, pl.ds(rf, bsz), 0:half] = s_re[:, 0:half]
            carry_ref[g, pl.ds(rb, bsz), half:2 * half] = s_re[:, half:]
            carry_ref[g, pl.ds(rf, bsz), 2 * half:3 * half] = s_im[:, 0:half]
            carry_ref[g, pl.ds(rb, bsz), 3 * half:] = s_im[:, half:]
            l_re = jnp.where(is_fwd, loc_ref[g, pl.ds(rf, bsz), 0:2 * half],
                             loc_ref[g, pl.ds(rb, bsz), 0:2 * half])
            l_im = jnp.where(is_fwd, loc_ref[g, pl.ds(rf, bsz), 2 * half:],
                             loc_ref[g, pl.ds(rb, bsz), 2 * half:])
            new.append(a_re[g] * s_re - a_im[g] * s_im + l_re)
            new.append(a_re[g] * s_im + a_im[g] * s_re + l_im)
        return tuple(new)

    for g in range(n_grp):
        intra_ref[g] = jnp.dot(mt_ref[g], z_ref[g], preferred_element_type=_F32)
    state = (jnp.zeros((bsz, 2 * half), _F32),) * (2 * n_grp)
    for k in range(n_chunks):
        state = step(k, state)
    for g in range(n_grp):
        y = intra_ref[g] + lax.dot_general(ct_ref[g], carry_ref[g].astype(_BF),
                                           (((1,), (1,)), ((), ())), preferred_element_type=_F32)
        y_ref[g] = y.astype(_BF)


def _s5_glu_kernel(x_ref, yt_ref, mod_ref, d_ref, wglu_ref, g_ref, b_ref, o_ref):
    sh = mod_ref[0]
    sc = 1.0 + mod_ref[1]
    gate = 1.0 + mod_ref[2]
    ncb, ntok, bsz, _ = x_ref.shape
    cols = ncb * bsz
    d_sc, d_sh = d_ref[...] * sc, d_ref[...] * sh
    for t0 in range(0, ntok, GLU_SUB):
        zs = []
        for tq in range(t0, t0 + GLU_SUB):
            yd = yt_ref[:, SSM_GROUP * tq:SSM_GROUP * (tq + 1), :].astype(_F32)
            yd = yd.reshape(D_MODEL, cols).T.reshape(ncb, bsz, D_MODEL)
            y = x_ref[:, tq] * d_sc[None] + (d_sh[None] + yd)
            zs.append(_gelu(y).reshape(cols, D_MODEL).astype(_BF))
        pz = jnp.dot(jnp.concatenate(zs, axis=0), wglu_ref[...], preferred_element_type=_F32)
        m = pz[:, :D_MODEL] * jax.nn.sigmoid(pz[:, D_MODEL:])
        for k in range(GLU_SUB):
            mt = m[cols * k:cols * (k + 1)].reshape(ncb, bsz, D_MODEL)
            o_ref[:, t0 + k] = _residual_layer_norm(x_ref[:, t0 + k], gate[None], mt,
                                                    g_ref[...], b_ref[...])


def _s5_mixer_layer(xt, mod, p, i):
    j = i // 2
    seq_len, bsz, _ = xt.shape
    n_chunks = seq_len // CHUNK
    ncb = LANES // bsz
    cols = n_chunks * bsz
    x4 = xt.reshape(n_chunks, CHUNK, bsz, D_MODEL)
    mod_full = pl.BlockSpec((6, bsz, D_MODEL), lambda *_: (0, 0, 0))
    zt = pl.pallas_call(
        _s5_regroup_kernel,
        grid=(n_chunks // ncb,),
        in_specs=[pl.BlockSpec((ncb, CHUNK, bsz, D_MODEL), lambda i: (i, 0, 0, 0)), mod_full],
        out_specs=pl.BlockSpec((SSM_GROUPS, CHUNK_COLS, LANES), lambda i: (0, 0, i)),
        out_shape=jax.ShapeDtypeStruct((SSM_GROUPS, CHUNK_COLS, cols), _BF),
        compiler_params=_params("parallel"),
        name="s5_regroup",
    )(x4, mod)
    per_group = cols * CHUNK_COLS * (2 * 2 + 2 * 2 + 3 * 4)
    gs = min(S5_MAX_GROUPS_PER_STEP, 1 << ((S5_CHUNK_VMEM_BUDGET // per_group).bit_length() - 1))
    grp = lambda r, c: pl.BlockSpec((gs, r, c), lambda g: (g, 0, 0))
    lay0 = j * (SSM_GROUPS // gs)
    tab = lambda r, c: pl.BlockSpec((gs, r, c), lambda g: (lay0 + g, 0, 0))
    yt = pl.pallas_call(
        functools.partial(_s5_chunk_kernel, bsz, n_chunks),
        grid=(SSM_GROUPS // gs,),
        in_specs=[grp(CHUNK_COLS, cols), tab(CHUNK_COLS, CHUNK_COLS), tab(CHUNK_COLS, CHUNK_COLS),
                  tab(CHUNK_COLS, CHUNK_COLS), tab(2, 2 * SSM_STATE)],
        out_specs=grp(CHUNK_COLS, cols),
        out_shape=jax.ShapeDtypeStruct((SSM_GROUPS, CHUNK_COLS, cols), _BF),
        scratch_shapes=[pltpu.VMEM((gs, cols, CHUNK_COLS), _F32),
                        pltpu.VMEM((gs, cols, CHUNK_COLS), _F32),
                        pltpu.VMEM((gs, CHUNK_COLS, cols), _F32)],
        compiler_params=_params("parallel"),
        name="s5_chunk_scan",
    )(zt, *p["s5_tables"])
    xblk = pl.BlockSpec((ncb, GLU_TOKS, bsz, D_MODEL), lambda i, q: (i, q, 0, 0))
    out = pl.pallas_call(
        _s5_glu_kernel,
        grid=(n_chunks // ncb, CHUNK // GLU_TOKS),
        in_specs=[xblk,
                  pl.BlockSpec((SSM_GROUPS, SSM_GROUP * GLU_TOKS, LANES), lambda i, q: (0, q, i)),
                  mod_full, _layer_spec(j, (1, D_MODEL)), _layer_spec(j, (D_MODEL, 2 * D_MODEL)),
                  _layer_spec(i, (1, D_MODEL)), _layer_spec(i, (1, D_MODEL))],
        out_specs=xblk,
        out_shape=jax.ShapeDtypeStruct(x4.shape, _F32),
        compiler_params=_params("parallel", "parallel"),
        name="s5_glu_layer",
    )(x4, yt, mod, p["s5_d"], p["s5_w_glu"], p["ln1_g"], p["ln1_b"])
    return out.reshape(xt.shape)


def _trunk(x, mod, p):
    xt = x
    for i in range(DEPTH):
        if i % 2 == 0:
            xt = _conv_mixer_layer(xt, mod[i], p, i, seq_major_in=(i == 0))
        else:
            xt = _s5_mixer_layer(xt, mod[i], p, i)
        xt = _ffn_layer(xt, mod[i], p, i, seq_major_out=(i == DEPTH - 1))
    return xt


def kernel(x_prompt, x_sample, c_prompt, c_sample, ada_w, ada_b, ln1_g, ln1_b, ln2_g, ln2_b, sc_w_in, sc_conv_w, sc_conv_b, sc_w_out, s5_a_re, s5_a_im, s5_log_dt, s5_b_re, s5_b_im, s5_c_re, s5_c_im, s5_d, s5_w_glu, ffn_w_up, ffn_conv_w, ffn_conv_b, ffn_w_down):
    row = lambda v: v[:, None, :]
    p = dict(ln1_g=row(ln1_g), ln1_b=row(ln1_b), ln2_g=row(ln2_g), ln2_b=row(ln2_b),
             sc_w_in=sc_w_in.astype(_BF), sc_conv_w=sc_conv_w, sc_conv_b=row(sc_conv_b),
             sc_w_out=sc_w_out.astype(_BF), s5_d=row(s5_d), s5_w_glu=s5_w_glu.astype(_BF),
             ffn_w_up=ffn_w_up.astype(_BF), ffn_conv_w=ffn_conv_w, ffn_conv_b=row(ffn_conv_b),
             ffn_w_down=ffn_w_down.astype(_BF),
             s5_tables=_s5_tables(s5_a_re, s5_a_im, s5_log_dt, s5_b_re, s5_b_im, s5_c_re, s5_c_im))
    n_prompt = c_prompt.shape[0]
    mod = _ada(jnp.concatenate([c_prompt, c_sample], axis=0), ada_w, ada_b)
    y_prompt = _trunk(x_prompt, mod[:, :, :n_prompt], p)
    y_sample = _trunk(x_sample, mod[:, :, n_prompt:], p)
    return (y_prompt, y_sample)
```

```python
import functools
import math

import jax
import jax.numpy as jnp
from jax import lax
from jax.experimental import pallas as pl
from jax.experimental.pallas import tpu as pltpu

D_MODEL = 1024
DEPTH = 4
FFN_HIDDEN = 2816
SSM_GROUP = 16
SSM_GROUPS = D_MODEL // SSM_GROUP
SSM_STATE = 64
DEEPNORM_ALPHA = (2 * DEPTH) ** 0.25
LN_EPS = 1e-5

SUBLANES = 8
LANES = 128
CHUNK = 16
CHUNK_COLS = CHUNK * SSM_GROUP
assert CHUNK & (CHUNK - 1) == 0
TOK_TILE = 128
GLU_TOKS = 16
GLU_SUB = 2
S5_MAX_GROUPS_PER_STEP = 8
S5_CHUNK_VMEM_BUDGET = 44 * 1024 * 1024
TABLE_GROUPS_PER_STEP = 4
FFN_COL_CHUNKS = (1280, 1536)
ROW_SPLITS = 4
VMEM_LIMIT = 56 * 1024 * 1024

_BF = jnp.bfloat16
_F32 = jnp.float32


def _layer_spec(idx, shape):
    nd = len(shape)
    return pl.BlockSpec((None,) + shape, lambda *_: (idx,) + (0,) * nd, pipeline_mode=pl.Buffered(1))


def _params(*sem):
    return pltpu.CompilerParams(dimension_semantics=sem, vmem_limit_bytes=VMEM_LIMIT)


_GELU_C1 = math.sqrt(2.0 / math.pi)
_GELU_C2 = _GELU_C1 * 0.044715


def _gelu_half_gate(x):
    return 0.5 + 0.5 * jnp.tanh(x * (_GELU_C1 + _GELU_C2 * (x * x)))


def _gelu(x):
    return x * _gelu_half_gate(x)


def _gelu_times(x, v):
    return (x * v) * _gelu_half_gate(x)


def _dot_bf16x3(a, b):
    a_hi, b_hi = a.astype(_BF), b.astype(_BF)
    a_lo, b_lo = (a - a_hi.astype(_F32)).astype(_BF), (b - b_hi.astype(_F32)).astype(_BF)
    d = functools.partial(jnp.dot, preferred_element_type=_F32)
    return d(a_hi, b_hi) + (d(a_hi, b_lo) + d(a_lo, b_hi))


def _residual_layer_norm(x, gate, m, g, b):
    h = x + (gate * (1.0 / DEEPNORM_ALPHA)) * m
    mu = jnp.mean(h, axis=-1, keepdims=True)
    d = h - mu
    var = jnp.mean(d * d, axis=-1, keepdims=True)
    return d * lax.rsqrt(var + LN_EPS / DEEPNORM_ALPHA ** 2) * g + b


def _conv3_rows(cur, before, after, w, b):
    up = jnp.concatenate([before, cur[:-SUBLANES]], axis=0)
    dn = jnp.concatenate([cur[SUBLANES:], after], axis=0)
    return up * w[0:1, :] + cur * w[1:2, :] + dn * w[2:3, :] + b


def _ada_kernel(c_ref, w_ref, b_ref, o_ref):
    c = c_ref[...]
    ca = c * jax.nn.sigmoid(c)
    o_ref[...] = _dot_bf16x3(ca, w_ref[...]) + b_ref[...]


def _ada(c, ada_w, ada_b):
    bsz = c.shape[0]
    return pl.pallas_call(
        _ada_kernel,
        grid=(DEPTH, 6),
        in_specs=[
            pl.BlockSpec((bsz, D_MODEL), lambda i, n: (0, 0)),
            pl.BlockSpec((None, D_MODEL, D_MODEL), lambda i, n: (i, 0, n)),
            pl.BlockSpec((None, 1, D_MODEL), lambda i, n: (i, 0, n)),
        ],
        out_specs=pl.BlockSpec((None, None, bsz, D_MODEL), lambda i, n: (i, n, 0, 0)),
        out_shape=jax.ShapeDtypeStruct((DEPTH, 6, bsz, D_MODEL), _F32),
        compiler_params=_params("arbitrary", "arbitrary"),
        name="ada_mod",
    )(c, ada_w, ada_b.reshape(DEPTH, 1, 6 * D_MODEL))


def _tile_specs(seq_len):
    main = pl.BlockSpec((TOK_TILE, SUBLANES, D_MODEL), lambda b, t: (t, b, 0))
    prev = pl.BlockSpec((1, SUBLANES, D_MODEL),
                        lambda b, t: (jnp.maximum(t * TOK_TILE - 1, 0), b, 0))
    nxt = pl.BlockSpec((1, SUBLANES, D_MODEL),
                       lambda b, t: (jnp.minimum((t + 1) * TOK_TILE, seq_len - 1), b, 0))
    return main, prev, nxt


def _mod_spec():
    return pl.BlockSpec((6, SUBLANES, D_MODEL), lambda b, t: (0, b, 0))


def _modulated_rows(xa, sc, sh):
    u = xa * sc[None] + sh[None]
    return u.reshape(xa.shape[0] * SUBLANES, D_MODEL).astype(_BF)


def _linear_step():
    return pl.program_id(0) * pl.num_programs(1) + pl.program_id(1)


def _gather_copies(x_hbm, buf, sem, slot, step, seq_len):
    tiles = seq_len // TOK_TILE
    seq0 = (step // tiles) * SUBLANES
    t0 = (step % tiles) * TOK_TILE
    spans = ((jnp.maximum(t0 - 1, 0), 1, 0), (t0, TOK_TILE, 1),
             (jnp.minimum(t0 + TOK_TILE, seq_len - 1), 1, TOK_TILE + 1))
    return [pltpu.make_async_copy(x_hbm.at[seq0 + b, pl.ds(src, n), :],
                                  buf.at[slot, pl.ds(dst, n), b, :], sem.at[slot])
            for b in range(SUBLANES) for src, n, dst in spans]


def _scatter_copies(buf, o_hbm, sem, slot, step, seq_len):
    tiles = seq_len // TOK_TILE
    seq0 = (step // tiles) * SUBLANES
    t0 = (step % tiles) * TOK_TILE
    return [pltpu.make_async_copy(buf.at[slot, :, b, :], o_hbm.at[seq0 + b, pl.ds(t0, TOK_TILE), :],
                                  sem.at[slot])
            for b in range(SUBLANES)]


def _gathered_tile(x_hbm, buf, sem, seq_len, n_steps):
    step = _linear_step()
    slot = step % 2

    @pl.when(step == 0)
    def _():
        for c in _gather_copies(x_hbm, buf, sem, slot, step, seq_len):
            c.start()

    @pl.when(step + 1 < n_steps)
    def _():
        for c in _gather_copies(x_hbm, buf, sem, 1 - slot, step + 1, seq_len):
            c.start()

    for c in _gather_copies(x_hbm, buf, sem, slot, step, seq_len):
        c.wait()
    return buf[slot]


def _scatter_slot(buf, o_hbm, sem, seq_len):
    step = _linear_step()
    slot = step % 2

    @pl.when(step >= 2)
    def _():
        for c in _scatter_copies(buf, o_hbm, sem, slot, step - 2, seq_len):
            c.wait()

    return buf.at[slot]


def _scatter_start(buf, o_hbm, sem, seq_len, n_steps):
    step = _linear_step()
    slot = step % 2
    for c in _scatter_copies(buf, o_hbm, sem, slot, step, seq_len):
        c.start()

    @pl.when(step == n_steps - 1)
    def _():
        if n_steps > 1:
            for c in _scatter_copies(buf, o_hbm, sem, 1 - slot, step - 1, seq_len):
                c.wait()
        for c in _scatter_copies(buf, o_hbm, sem, slot, step, seq_len):
            c.wait()


def _conv3_tile(full, first, last, w, b):
    n = full.shape[0]
    zero = jnp.zeros((SUBLANES, full.shape[1]), _F32)
    return _conv3_rows(full[SUBLANES:n - SUBLANES], jnp.where(first, zero, full[:SUBLANES]),
                       jnp.where(last, zero, full[n - SUBLANES:]), w, b)


def _conv_mixer_kernel(gather, *refs):
    if gather:
        x_hbm, mod_ref, win_ref, cw_ref, cb_ref, wout_ref, g_ref, b_ref, o_ref, buf, sem = refs
        xa = _gathered_tile(x_hbm, buf, sem, *gather)
    else:
        (x_ref, xp_ref, xn_ref, mod_ref, win_ref, cw_ref, cb_ref, wout_ref, g_ref, b_ref,
         o_ref) = refs
        xa = jnp.concatenate([xp_ref[...], x_ref[...], xn_ref[...]], axis=0)
    t = pl.program_id(1)
    first = t == 0
    last = t == pl.num_programs(1) - 1
    sh = mod_ref[0]
    sc = 1.0 + mod_ref[1]
    gate = 1.0 + mod_ref[2]
    x = xa[1:xa.shape[0] - 1]
    u = _modulated_rows(xa, sc, sh)
    q = (jnp.dot(u, win_ref[:, D_MODEL:2 * D_MODEL], preferred_element_type=_F32)
         * jnp.dot(u, win_ref[:, 2 * D_MODEL:], preferred_element_type=_F32))
    cq = _conv3_tile(q, first, last, cw_ref[...], cb_ref[...])
    bgate = jnp.dot(u[SUBLANES:u.shape[0] - SUBLANES], win_ref[:, :D_MODEL],
                    preferred_element_type=_F32)
    r = (bgate * cq).astype(_BF)
    tok = x.shape[0] // ROW_SPLITS
    rows = tok * SUBLANES
    for i in range(ROW_SPLITS):
        xr = x[i * tok:(i + 1) * tok]
        m = jnp.dot(r[i * rows:(i + 1) * rows], wout_ref[...], preferred_element_type=_F32)
        o_ref[i * tok:(i + 1) * tok] = _residual_layer_norm(
            xr, gate[None], m.reshape(xr.shape), g_ref[...], b_ref[...])


def _conv_mixer_layer(x, mod, p, i, seq_major_in=False):
    j = i // 2
    if seq_major_in:
        bsz, seq_len, _ = x.shape
    else:
        seq_len, bsz, _ = x.shape
    grid = (bsz // SUBLANES, seq_len // TOK_TILE)
    main, prev, nxt = _tile_specs(seq_len)
    weights = [_layer_spec(j, (D_MODEL, 3 * D_MODEL)), _layer_spec(j, (3, D_MODEL)),
               _layer_spec(j, (1, D_MODEL)), _layer_spec(j, (D_MODEL, D_MODEL)),
               _layer_spec(i, (1, D_MODEL)), _layer_spec(i, (1, D_MODEL))]
    if seq_major_in:
        gather = (seq_len, grid[0] * grid[1])
        x_specs, x_args = [pl.BlockSpec(memory_space=pl.ANY)], (x,)
        scratch = [pltpu.VMEM((2, TOK_TILE + 2, SUBLANES, D_MODEL), _F32),
                   pltpu.SemaphoreType.DMA((2,))]
        sem = ("arbitrary", "arbitrary")
    else:
        gather = None
        x_specs, x_args = [main, prev, nxt], (x, x, x)
        scratch = []
        sem = ("parallel", "parallel")
    return pl.pallas_call(
        functools.partial(_conv_mixer_kernel, gather),
        grid=grid,
        in_specs=x_specs + [_mod_spec()] + weights,
        out_specs=main,
        out_shape=jax.ShapeDtypeStruct((seq_len, bsz, D_MODEL), _F32),
        scratch_shapes=scratch,
        compiler_params=_params(*sem),
        name="conv_mixer_layer",
    )(*x_args, mod, p["sc_w_in"], p["sc_conv_w"], p["sc_conv_b"], p["sc_w_out"],
      p["ln1_g"], p["ln1_b"])


def _ffn_kernel(scatter, x_ref, xp_ref, xn_ref, mod_ref, wup_ref, cw_ref, cb_ref, wdn_ref, g_ref,
                b_ref, o_ref, *dma_scratch):
    dst_ref = _scatter_slot(dma_scratch[0], o_ref, dma_scratch[1], scatter[0]) if scatter else o_ref
    t = pl.program_id(1)
    first = t == 0
    last = t == pl.num_programs(1) - 1
    sh = mod_ref[3]
    sc = 1.0 + mod_ref[4]
    gate = 1.0 + mod_ref[5]
    x = x_ref[...]
    u = _modulated_rows(jnp.concatenate([xp_ref[...], x, xn_ref[...]], axis=0), sc, sh)
    tok = x.shape[0] // ROW_SPLITS
    rows = tok * SUBLANES
    accs = [None] * ROW_SPLITS
    k0 = 0
    for kc in FFN_COL_CHUNKS:
        halves = []
        for base in (0, FFN_HIDDEN):
            lo = base + k0
            a = jnp.dot(u, wup_ref[:, lo:lo + kc], preferred_element_type=_F32)
            halves.append(_conv3_tile(a, first, last, cw_ref[:, lo:lo + kc], cb_ref[:, lo:lo + kc]))
        hid = _gelu_times(halves[0], halves[1]).astype(_BF)
        for r in range(ROW_SPLITS):
            part = jnp.dot(hid[r * rows:(r + 1) * rows], wdn_ref[k0:k0 + kc, :],
                           preferred_element_type=_F32)
            accs[r] = part if accs[r] is None else accs[r] + part
        k0 += kc
    for r in range(ROW_SPLITS):
        xr = x[r * tok:(r + 1) * tok]
        dst_ref[r * tok:(r + 1) * tok] = _residual_layer_norm(
            xr, gate[None], accs[r].reshape(xr.shape), g_ref[...], b_ref[...])
    if scatter:
        _scatter_start(dma_scratch[0], o_ref, dma_scratch[1], *scatter)


def _ffn_layer(xt, mod, p, i, seq_major_out=False):
    seq_len, bsz, _ = xt.shape
    grid = (bsz // SUBLANES, seq_len // TOK_TILE)
    main, prev, nxt = _tile_specs(seq_len)
    if seq_major_out:
        scatter = (seq_len, grid[0] * grid[1])
        out_spec = pl.BlockSpec(memory_space=pl.ANY)
        out_shape = (bsz, seq_len, D_MODEL)
        scratch = [pltpu.VMEM((2, TOK_TILE, SUBLANES, D_MODEL), _F32), pltpu.SemaphoreType.DMA((2,))]
        sem = ("arbitrary", "arbitrary")
    else:
        scatter = None
        out_spec, out_shape, scratch = main, xt.shape, []
        sem = ("parallel", "parallel")
    return pl.pallas_call(
        functools.partial(_ffn_kernel, scatter),
        grid=grid,
        in_specs=[main, prev, nxt, _mod_spec(),
                  _layer_spec(i, (D_MODEL, 2 * FFN_HIDDEN)), _layer_spec(i, (3, 2 * FFN_HIDDEN)),
                  _layer_spec(i, (1, 2 * FFN_HIDDEN)), _layer_spec(i, (FFN_HIDDEN, D_MODEL)),
                  _layer_spec(i, (1, D_MODEL)), _layer_spec(i, (1, D_MODEL))],
        out_specs=out_spec,
        out_shape=jax.ShapeDtypeStruct(out_shape, _F32),
        scratch_shapes=scratch,
        compiler_params=_params(*sem),
        name="conv_ffn_layer",
    )(xt, xt, xt, mod, p["ffn_w_up"], p["ffn_conv_w"], p["ffn_conv_b"], p["ffn_w_down"],
      p["ln2_g"], p["ln2_b"])


def _s5_tables(a_re, a_im, log_dt, b_re, b_im, c_re, c_im):
    n_lay, n_dir, n_grp, n_st = a_re.shape
    both = n_dir * n_st
    rows = n_lay * n_grp

    def states_last(v):
        return v.transpose(0, 2, 3, 1, 4).reshape(rows, v.shape[3], both)

    ldt = jnp.broadcast_to(log_dt[..., None], a_re.shape)
    par = states_last(jnp.stack([a_re, a_im, ldt], axis=3))
    b_col = jnp.stack([b_re, b_im], axis=2)
    b_col = b_col.transpose(0, 3, 2, 1, 4, 5).reshape(rows, 2, both, SSM_GROUP)
    b_row = jnp.stack([states_last(b_re.transpose(0, 1, 2, 4, 3)),
                       states_last(b_im.transpose(0, 1, 2, 4, 3))], axis=1)
    c_row = jnp.stack([states_last(c_re), states_last(c_im)], axis=1)
    blk = lambda *shape: pl.BlockSpec((TABLE_GROUPS_PER_STEP,) + shape,
                                      lambda g: (g,) + (0,) * len(shape))
    sq = (CHUNK_COLS, CHUNK_COLS)
    return pl.pallas_call(
        _s5_table_kernel,
        grid=(rows // TABLE_GROUPS_PER_STEP,),
        in_specs=[blk(3, both), blk(2, both, SSM_GROUP), blk(2, SSM_GROUP, both),
                  blk(2, SSM_GROUP, both)],
        out_specs=[blk(*sq), blk(*sq), blk(*sq), blk(2, both)],
        out_shape=[jax.ShapeDtypeStruct((rows,) + sq, _BF)] * 3
        + [jax.ShapeDtypeStruct((rows, 2, both), _F32)],
        compiler_params=_params("parallel"),
        name="s5_tables",
    )(par, b_col, b_row, c_row)


def _s5_table_kernel(*refs):
    for g in range(TABLE_GROUPS_PER_STEP):
        _s5_group_tables(*(r.at[g] for r in refs))


def _s5_group_tables(par_ref, bcol_ref, brow_ref, crow_ref, mt_ref, bt_ref, ct_ref, dec_ref):
    tt = CHUNK
    half = SSM_STATE
    both = 2 * half

    def discretise(a_re, a_im, log_dt):
        dt = jnp.exp(log_dt)
        mag = jnp.exp(a_re * dt)
        lb_re, lb_im = mag * jnp.cos(a_im * dt), mag * jnp.sin(a_im * dt)
        den = a_re * a_re + a_im * a_im
        f_re = ((lb_re - 1.0) * a_re + lb_im * a_im) / den
        f_im = (lb_im * a_re - (lb_re - 1.0) * a_im) / den
        squares = [(lb_re, lb_im)]
        while len(squares) <= tt.bit_length() - 1:
            r, i = squares[-1]
            squares.append((r * r - i * i, 2.0 * (r * i)))

        def power(n):
            p_re = jnp.ones(n.shape, _F32)
            p_im = jnp.zeros(n.shape, _F32)
            for k, (s_re, s_im) in enumerate(squares):
                bit = ((n >> k) & 1) == 1
                p_re, p_im = (jnp.where(bit, p_re * s_re - p_im * s_im, p_re),
                              jnp.where(bit, p_re * s_im + p_im * s_re, p_im))
            return p_re, p_im

        return power, squares[-1], f_re, f_im

    power, lb_chunk, f_re, f_im = discretise(par_ref[0:1, :], par_ref[1:2, :], par_ref[2:3, :])
    tok = lax.broadcasted_iota(jnp.int32, (tt, both), 0)
    fwd = lax.broadcasted_iota(jnp.int32, (tt, both), 1) < half
    p_re, p_im = power(jnp.where(fwd, tt - 1 - tok, tok))
    w_re = (p_re * f_re - p_im * f_im)[:, None, :]
    w_im = (p_re * f_im + p_im * f_re)[:, None, :]
    b_re, b_im = brow_ref[0][None], brow_ref[1][None]
    bt_ref[:, 0:both] = (w_re * b_re - w_im * b_im).reshape(CHUNK_COLS, both).astype(_BF)
    bt_ref[:, both:] = (w_re * b_im + w_im * b_re).reshape(CHUNK_COLS, both).astype(_BF)
    e_re, e_im = power(jnp.where(fwd, tok + 1, tt - tok))
    e_re, e_im = e_re[:, None, :], e_im[:, None, :]
    c_re, c_im = crow_ref[0], crow_ref[1]
    ct_ref[:, 0:both] = (c_re[None] * e_re - c_im[None] * e_im).reshape(CHUNK_COLS, both).astype(_BF)
    ct_ref[:, both:] = (-(c_re[None] * e_im + c_im[None] * e_re)).reshape(CHUNK_COLS, both).astype(_BF)
    dec_ref[0:1, :] = lb_chunk[0]
    dec_ref[1:2, :] = lb_chunk[1]

    n_s = 2 * tt
    s_idx = lax.broadcasted_iota(jnp.int32, (n_s, both), 0)
    fwd = lax.broadcasted_iota(jnp.int32, (n_s, both), 1) < half
    expo = jnp.where(fwd, tt - 1 - s_idx, s_idx - (tt - 1))
    live = expo >= 0
    p_re, p_im = power(jnp.maximum(expo, 0))
    w_re = jnp.where(live, p_re * f_re - p_im * f_im, 0.0).T
    w_im = jnp.where(live, p_re * f_im + p_im * f_re, 0.0).T
    width = n_s * SSM_GROUP

    def spread(v, pick):
        k = v.shape[1]
        row = lax.broadcasted_iota(jnp.int32, (3 * k, width), 0)
        onehot = (pick(lax.broadcasted_iota(jnp.int32, (3 * k, width), 1)) == row % k).astype(_BF)
        hi = v.astype(_BF)
        rest = v - hi.astype(_F32)
        mid = rest.astype(_BF)
        lo = (rest - mid.astype(_F32)).astype(_BF)
        return jnp.dot(jnp.concatenate([hi, mid, lo], axis=1), onehot, preferred_element_type=_F32)

    w_re, w_im = (spread(w, lambda l: l // SSM_GROUP) for w in (w_re, w_im))
    b_re, b_im = (spread(bcol_ref[k], lambda l: l % SSM_GROUP) for k in (0, 1))
    strip = (_dot_bf16x3(c_re, w_re * b_re - w_im * b_im)
             - _dot_bf16x3(c_im, w_re * b_im + w_im * b_re))
    for t in range(tt):
        lo = SSM_GROUP * (tt - 1 - t)
        mt_ref[SSM_GROUP * t:SSM_GROUP * (t + 1), :] = strip[:, lo:lo + CHUNK_COLS].astype(_BF)


def _s5_regroup_kernel(x_ref, mod_ref, o_ref):
    sh = mod_ref[0]
    sc = 1.0 + mod_ref[1]
    cols = x_ref.shape[0] * x_ref.shape[2]
    for tau in range(CHUNK):
        slab = (x_ref[:, tau] * sc[None] + sh[None]).reshape(cols, D_MODEL)
        o_ref[:, SSM_GROUP * tau:SSM_GROUP * (tau + 1), :] = (
            slab.T.astype(_BF).reshape(SSM_GROUPS, SSM_GROUP, cols))


def _s5_chunk_kernel(bsz, n_chunks, z_ref, mt_ref, bt_ref, ct_ref, dec_ref, y_ref,
                     loc_ref, carry_ref, intra_ref):
    half = SSM_STATE
    n_grp = z_ref.shape[0]
    for g in range(n_grp):
        loc_ref[g] = lax.dot_general(z_ref[g], bt_ref[g], (((0,), (0,)), ((), ())),
                                     preferred_element_type=_F32)
    a_re = [jnp.broadcast_to(dec_ref[g, 0:1, :], (bsz, 2 * half)) for g in range(n_grp)]
    a_im = [jnp.broadcast_to(dec_ref[g, 1:2, :], (bsz, 2 * half)) for g in range(n_grp)]
    is_fwd = lax.broadcasted_iota(jnp.int32, (bsz, 2 * half), 1) < half

    def step(k, state):
        rf = k * bsz
        rb = (n_chunks - 1 - k) * bsz
        new = []
        for g in range(n_grp):
            s_re, s_im = state[2 * g], state[2 * g + 1]
            carry_ref[g, pl.ds(rf, bsz), 0:half] = s_re[:, 0:half]
            carry_ref[g, pl.ds(rb, bsz), half:2 * half] = s_re[:, half:]
            carry_ref[g, pl.ds(rf, bsz), 2 * half:3 * half] = s_im[:, 0:half]
            carry_ref[g, pl.ds(rb, bsz), 3 * half:] = s_im[:, half:]
            l_re = jnp.where(is_fwd, loc_ref[g, pl.ds(rf, bsz), 0:2 * half],
                             loc_ref[g, pl.ds(rb, bsz), 0:2 * half])
            l_im = jnp.where(is_fwd, loc_ref[g, pl.ds(rf, bsz), 2 * half:],
                             loc_ref[g, pl.ds(rb, bsz), 2 * half:])
            new.append(a_re[g] * s_re - a_im[g] * s_im + l_re)
            new.append(a_re[g] * s_im + a_im[g] * s_re + l_im)
        return tuple(new)

    for g in range(n_grp):
        intra_ref[g] = jnp.dot(mt_ref[g], z_ref[g], preferred_element_type=_F32)
    state = (jnp.zeros((bsz, 2 * half), _F32),) * (2 * n_grp)
    for k in range(n_chunks):
        state = step(k, state)
    for g in range(n_grp):
        y = intra_ref[g] + lax.dot_general(ct_ref[g], carry_ref[g].astype(_BF),
                                           (((1,), (1,)), ((), ())), preferred_element_type=_F32)
        y_ref[g] = y.astype(_BF)


def _s5_glu_kernel(x_ref, yt_ref, mod_ref, d_ref, wglu_ref, g_ref, b_ref, o_ref):
    sh = mod_ref[0]
    sc = 1.0 + mod_ref[1]
    gate = 1.0 + mod_ref[2]
    ncb, ntok, bsz, _ = x_ref.shape
    cols = ncb * bsz
    d_sc, d_sh = d_ref[...] * sc, d_ref[...] * sh
    for t0 in range(0, ntok, GLU_SUB):
        zs = []
        for tq in range(t0, t0 + GLU_SUB):
            yd = yt_ref[:, SSM_GROUP * tq:SSM_GROUP * (tq + 1), :].astype(_F32)
            yd = yd.reshape(D_MODEL, cols).T.reshape(ncb, bsz, D_MODEL)
            y = x_ref[:, tq] * d_sc[None] + (d_sh[None] + yd)
            zs.append(_gelu(y).reshape(cols, D_MODEL).astype(_BF))
        pz = jnp.dot(jnp.concatenate(zs, axis=0), wglu_ref[...], preferred_element_type=_F32)
        m = pz[:, :D_MODEL] * jax.nn.sigmoid(pz[:, D_MODEL:])
        for k in range(GLU_SUB):
            mt = m[cols * k:cols * (k + 1)].reshape(ncb, bsz, D_MODEL)
            o_ref[:, t0 + k] = _residual_layer_norm(x_ref[:, t0 + k], gate[None], mt,
                                                    g_ref[...], b_ref[...])


def _s5_mixer_layer(xt, mod, p, i):
    j = i // 2
    seq_len, bsz, _ = xt.shape
    n_chunks = seq_len // CHUNK
    ncb = LANES // bsz
    cols = n_chunks * bsz
    x4 = xt.reshape(n_chunks, CHUNK, bsz, D_MODEL)
    mod_full = pl.BlockSpec((6, bsz, D_MODEL), lambda *_: (0, 0, 0))
    zt = pl.pallas_call(
        _s5_regroup_kernel,
        grid=(n_chunks // ncb,),
        in_specs=[pl.BlockSpec((ncb, CHUNK, bsz, D_MODEL), lambda i: (i, 0, 0, 0)), mod_full],
        out_specs=pl.BlockSpec((SSM_GROUPS, CHUNK_COLS, LANES), lambda i: (0, 0, i)),
        out_shape=jax.ShapeDtypeStruct((SSM_GROUPS, CHUNK_COLS, cols), _BF),
        compiler_params=_params("parallel"),
        name="s5_regroup",
    )(x4, mod)
    per_group = cols * CHUNK_COLS * (2 * 2 + 2 * 2 + 3 * 4)
    gs = min(S5_MAX_GROUPS_PER_STEP, 1 << ((S5_CHUNK_VMEM_BUDGET // per_group).bit_length() - 1))
    grp = lambda r, c: pl.BlockSpec((gs, r, c), lambda g: (g, 0, 0))
    lay0 = j * (SSM_GROUPS // gs)
    tab = lambda r, c: pl.BlockSpec((gs, r, c), lambda g: (lay0 + g, 0, 0))
    yt = pl.pallas_call(
        functools.partial(_s5_chunk_kernel, bsz, n_chunks),
        grid=(SSM_GROUPS // gs,),
        in_specs=[grp(CHUNK_COLS, cols), tab(CHUNK_COLS, CHUNK_COLS), tab(CHUNK_COLS, CHUNK_COLS),
                  tab(CHUNK_COLS, CHUNK_COLS), tab(2, 2 * SSM_STATE)],
        out_specs=grp(CHUNK_COLS, cols),
        out_shape=jax.ShapeDtypeStruct((SSM_GROUPS, CHUNK_COLS, cols), _BF),
        scratch_shapes=[pltpu.VMEM((gs, cols, CHUNK_COLS), _F32),
                        pltpu.VMEM((gs, cols, CHUNK_COLS), _F32),
                        pltpu.VMEM((gs, CHUNK_COLS, cols), _F32)],
        compiler_params=_params("parallel"),
        name="s5_chunk_scan",
    )(zt, *p["s5_tables"])
    xblk = pl.BlockSpec((ncb, GLU_TOKS, bsz, D_MODEL), lambda i, q: (i, q, 0, 0))
    out = pl.pallas_call(
        _s5_glu_kernel,
        grid=(n_chunks // ncb, CHUNK // GLU_TOKS),
        in_specs=[xblk,
                  pl.BlockSpec((SSM_GROUPS, SSM_GROUP * GLU_TOKS, LANES), lambda i, q: (0, q, i)),
                  mod_full, _layer_spec(j, (1, D_MODEL)), _layer_spec(j, (D_MODEL, 2 * D_MODEL)),
                  _layer_spec(i, (1, D_MODEL)), _layer_spec(i, (1, D_MODEL))],
        out_specs=xblk,
        out_shape=jax.ShapeDtypeStruct(x4.shape, _F32),
        compiler_params=_params("parallel", "parallel"),
        name="s5_glu_layer",
    )(x4, yt, mod, p["s5_d"], p["s5_w_glu"], p["ln1_g"], p["ln1_b"])
    return out.reshape(xt.shape)


def _trunk(x, mod, p):
    xt = x
    for i in range(DEPTH):
        if i % 2 == 0:
            xt = _conv_mixer_layer(xt, mod[i], p, i, seq_major_in=(i == 0))
        else:
            xt = _s5_mixer_layer(xt, mod[i], p, i)
        xt = _ffn_layer(xt, mod[i], p, i, seq_major_out=(i == DEPTH - 1))
    return xt


def kernel(x_prompt, x_sample, c_prompt, c_sample, ada_w, ada_b, ln1_g, ln1_b, ln2_g, ln2_b, sc_w_in, sc_conv_w, sc_conv_b, sc_w_out, s5_a_re, s5_a_im, s5_log_dt, s5_b_re, s5_b_im, s5_c_re, s5_c_im, s5_d, s5_w_glu, ffn_w_up, ffn_conv_w, ffn_conv_b, ffn_w_down):
    row = lambda v: v[:, None, :]
    p = dict(ln1_g=row(ln1_g), ln1_b=row(ln1_b), ln2_g=row(ln2_g), ln2_b=row(ln2_b),
             sc_w_in=sc_w_in.astype(_BF), sc_conv_w=sc_conv_w, sc_conv_b=row(sc_conv_b),
             sc_w_out=sc_w_out.astype(_BF), s5_d=row(s5_d), s5_w_glu=s5_w_glu.astype(_BF),
             ffn_w_up=ffn_w_up.astype(_BF), ffn_conv_w=ffn_conv_w, ffn_conv_b=row(ffn_conv_b),
             ffn_w_down=ffn_w_down.astype(_BF),
             s5_tables=_s5_tables(s5_a_re, s5_a_im, s5_log_dt, s5_b_re, s5_b_im, s5_c_re, s5_c_im))
    n_prompt = c_prompt.shape[0]
    mod = _ada(jnp.concatenate([c_prompt, c_sample], axis=0), ada_w, ada_b)
    y_prompt = _trunk(x_prompt, mod[:, :, :n_prompt], p)
    y_sample = _trunk(x_sample, mod[:, :, n_prompt:], p)
    return (y_prompt, y_sample)
```
